```python
import math
import jax, jax.numpy as jnp
from jax import lax
import numpy as np

D_MODEL = 2048
BATCH = 8
SEQ = 2048
DEPTH = 2
DEC_BATCH = 32
DEC_SEQ = 4
PAST_LEN = 8192
PAGE_SIZE = 128

f32 = jnp.float32
EPS = 1e-6
N_EVEN = (DEPTH + 1) // 2
N_ODD = DEPTH // 2

FOX_HD = 128
FOX_HEADS = D_MODEL // 2 // FOX_HD
FOX_W = FOX_HEADS * FOX_HD
FOX_Q_BLOCK = 128
HGRN_DK = 128
HGRN_HEADS = D_MODEL // 2 // 128
HGRN_DV = D_MODEL // 2 // HGRN_HEADS
HGRN_KW = HGRN_HEADS * HGRN_DK
HGRN_VW = HGRN_HEADS * HGRN_DV
HGRN_CHUNK = 64
WIDTHS_A = (FOX_W, FOX_W, FOX_W, FOX_HEADS, HGRN_KW, HGRN_KW, HGRN_VW, HGRN_VW)
IN_A = sum(WIDTHS_A)
MIX_A = FOX_W + HGRN_VW
SSD_INNER = 2 * D_MODEL
SSD_HD = 64
SSD_HEADS = SSD_INNER // SSD_HD
SSD_GROUPS = 8
SSD_R = SSD_HEADS // SSD_GROUPS
SSD_N = 128
SSD_CONV = 4
SSD_CONV_DIM = SSD_INNER + 2 * SSD_GROUPS * SSD_N
SSD_CHUNK = 64
IN_C = SSD_INNER + SSD_CONV_DIM + SSD_HEADS
PEER_HEADS = 8
PEER_NKEYS = 128
PEER_EXPERTS = PEER_NKEYS * PEER_NKEYS
PEER_DQ = 256
PEER_TOPK = 16
PEER_TOKEN_BLOCK = 128

kernel_name = 'fox_hgrn2_ssd_peer_hybrid_step'


def rmsnorm(x, w):
    xf = x.astype(f32)
    y = xf * lax.rsqrt(jnp.mean(xf * xf, axis=-1, keepdims=True) + EPS)
    return (y * w.astype(f32)).astype(x.dtype)


def split_cols(h, widths):
    offs, acc = [], 0
    for w in widths[:-1]:
        acc += w
        offs.append(acc)
    return jnp.split(h, offs, axis=-1)


def chunk_size(T, c):
    return c if T % c == 0 else T


def fox_block(q, k, v, cq, ck, qpos, kpos):
    s = jnp.einsum('bqhd,bshd->bhqs', q, k).astype(f32) * (q.shape[-1] ** -0.5)
    s = s + jnp.transpose(cq, (0, 2, 1))[..., None] - jnp.transpose(ck, (0, 2, 1))[:, :, None, :]
    s = jnp.where(kpos[None, :] <= qpos[:, None], s, -jnp.inf)
    p = jax.nn.softmax(s, axis=-1).astype(v.dtype)
    return jnp.einsum('bhqs,bshd->bqhd', p, v)


def fox_prompt(q, k, v, logf):
    B, T, H, Dh = q.shape
    c = jnp.cumsum(logf, axis=1)
    nb = T // FOX_Q_BLOCK
    qb = q.reshape(B, nb, FOX_Q_BLOCK, H, Dh).swapaxes(0, 1)
    cb = c.reshape(B, nb, FOX_Q_BLOCK, H).swapaxes(0, 1)
    pb = jnp.arange(T).reshape(nb, FOX_Q_BLOCK)
    kpos = jnp.arange(T)
    o = lax.map(lambda a: fox_block(a[0], k, v, a[1], c, a[2], kpos), (qb, cb, pb))
    return o.swapaxes(0, 1).reshape(B, T, H, Dh)


def fox_sample(q, k, v, logf, k_pool, v_pool, lf_pool, page_table):
    Bd, S, H, Dh = q.shape
    past = page_table.shape[1] * PAGE_SIZE
    k_all = jnp.concatenate([k_pool[page_table].reshape(Bd, past, H, Dh).astype(k.dtype), k], axis=1)
    v_all = jnp.concatenate([v_pool[page_table].reshape(Bd, past, H, Dh).astype(v.dtype), v], axis=1)
    lf_all = jnp.concatenate([lf_pool[page_table].reshape(Bd, past, H).astype(f32), logf], axis=1)
    c = jnp.cumsum(lf_all, axis=1)
    return fox_block(q, k_all, v_all, c[:, past:], c, past + jnp.arange(S), jnp.arange(past + S))


def gla_chunked(q, k, v, logf, S0):
    B, T, H, K = q.shape
    C = chunk_size(T, HGRN_CHUNK)
    n = T // C
    tc = lambda a: a.astype(f32).reshape(B, n, C, *a.shape[2:]).swapaxes(0, 1)
    causal = jnp.arange(C)[:, None] >= jnp.arange(C)[None, :]

    def step(S, inp):
        qc, kc, vc, lc = inp
        b = jnp.cumsum(lc, axis=1)
        diff = b[:, :, None] - b[:, None, :]
        decay = jnp.exp(jnp.where(causal[None, :, :, None, None], diff, -jnp.inf))
        A = jnp.einsum('bthk,btshk,bshk->bhts', qc, decay, kc)
        o = jnp.einsum('bhts,bshv->bthv', A, vc) + jnp.einsum('bthk,bhkv->bthv', qc * jnp.exp(b), S)
        bl = b[:, -1]
        S_new = S * jnp.exp(bl)[..., None] + jnp.einsum('bshk,bshv->bhkv', kc * jnp.exp(bl[:, None] - b), vc)
        return S_new, o

    S, o = lax.scan(step, S0.astype(f32), (tc(q), tc(k), tc(v), tc(logf)))
    return o.swapaxes(0, 1).reshape(B, T, H, v.shape[-1]), S


def ssd_chunked(x, dt, bm, cm, a_neg, S0):
    B, T = x.shape[:2]
    C = chunk_size(T, SSD_CHUNK)
    n = T // C
    tc = lambda a: a.astype(f32).reshape(B, n, C, *a.shape[2:]).swapaxes(0, 1)
    causal = jnp.arange(C)[:, None] >= jnp.arange(C)[None, :]
    la = dt.astype(f32) * a_neg.astype(f32)

    def step(S, inp):
        xc, dtc, lac, bc, cc = inp
        cs = jnp.cumsum(lac, axis=1)
        diff = cs[:, :, None] - cs[:, None, :]
        L = jnp.exp(jnp.where(causal[None, :, :, None, None], diff, -jnp.inf))
        cb = jnp.einsum('btgn,bsgn->btsg', cc, bc)
        W = cb[..., None] * L * dtc[:, None]
        y = jnp.einsum('btsgr,bsgrp->btgrp', W, xc)
        y = y + jnp.einsum('btgn,bgrpn->btgrp', cc, S) * jnp.exp(cs)[..., None]
        cl = cs[:, -1]
        w = dtc * jnp.exp(cl[:, None] - cs)
        S_new = S * jnp.exp(cl)[..., None, None] + jnp.einsum('bsgn,bsgr,bsgrp->bgrpn', bc, w, xc)
        return S_new, y

    S, y = lax.scan(step, S0.astype(f32), (tc(x), tc(dt), tc(la), tc(bm), tc(cm)))
    return y.swapaxes(0, 1).reshape(x.shape), S


def peer_ffn(xn, wq, keys, u, v):
    n, d = xn.shape
    nblk = -(-n // PEER_TOKEN_BLOCK)
    xp = jnp.pad(xn, ((0, nblk * PEER_TOKEN_BLOCK - n), (0, 0))).reshape(nblk, PEER_TOKEN_BLOCK, d)

    def block(xb):
        q = (xb @ wq).reshape(-1, PEER_HEADS, 2, PEER_DQ // 2)
        s = jnp.einsum('nhcd,hckd->nhck', q, keys).astype(f32)
        s1, i1 = lax.top_k(s[:, :, 0], PEER_TOPK)
        s2, i2 = lax.top_k(s[:, :, 1], PEER_TOPK)
        cand = (s1[..., :, None] + s2[..., None, :]).reshape(-1, PEER_HEADS, PEER_TOPK * PEER_TOPK)
        cidx = (i1[..., :, None] * PEER_NKEYS + i2[..., None, :]).reshape(-1, PEER_HEADS, PEER_TOPK * PEER_TOPK)
        top, pos = lax.top_k(cand, PEER_TOPK)
        eidx = jnp.take_along_axis(cidx, pos, axis=-1)
        g = jax.nn.softmax(top, axis=-1)
        hact = jax.nn.gelu(jnp.einsum('nd,nhkd->nhk', xb, u[eidx]).astype(f32), approximate=False)
        return jnp.einsum('nhk,nhkd->nd', (g * hact).astype(xb.dtype), v[eidx])

    return lax.map(block, xp).reshape(-1, d)[:n]


def trunk(x, fox_past, hgrn0, ssm0, conv0, params):
    (norm_mix, norm_ffn, norm_final, w_in_a, b_fox_f, hgrn_lb_logits, hgrn_norm_w, w_out_a,
     w_in_c, conv_w, conv_b, dt_bias, a_log, d_skip, ssd_norm_w, w_out_c,
     peer_wq, peer_keys, peer_u, peer_v) = params
    B, T, D = x.shape
    lb_all = jnp.cumsum(jax.nn.softmax(hgrn_lb_logits.astype(f32), axis=0), axis=0)
    fks, fvs, fls, hs, ss, cs = [], [], [], [], [], []
    for layer in range(DEPTH):
        j = layer // 2
        xn = rmsnorm(x, norm_mix[layer])
        if layer % 2 == 0:
            fq, fk, fv, ff, hq, hf, hi, hg = split_cols(xn @ w_in_a[j], WIDTHS_A)
            fq = fq.reshape(B, T, FOX_HEADS, FOX_HD)
            fk = fk.reshape(B, T, FOX_HEADS, FOX_HD)
            fv = fv.reshape(B, T, FOX_HEADS, FOX_HD)
            flogf = jax.nn.log_sigmoid((ff + b_fox_f[j]).astype(f32))
            if fox_past is None:
                fo = fox_prompt(fq, fk, fv, flogf)
            else:
                fo = fox_sample(fq, fk, fv, flogf, fox_past[0][j], fox_past[1][j], fox_past[2][j], fox_past[3])
            fks.append(fk); fvs.append(fv); fls.append(flogf)
            lb = lb_all[j].reshape(HGRN_HEADS, HGRN_DK)
            z = hf.reshape(B, T, HGRN_HEADS, HGRN_DK).astype(f32)
            hlogf = jnp.log(lb + (1.0 - lb) * jax.nn.sigmoid(z))
            hk = (1.0 - lb) * jax.nn.sigmoid(-z)
            ho, hS = gla_chunked(hq.reshape(B, T, HGRN_HEADS, HGRN_DK), hk,
                                 hi.reshape(B, T, HGRN_HEADS, HGRN_DV), hlogf, hgrn0[j])
            ho = rmsnorm(ho, hgrn_norm_w[j]) * jax.nn.silu(hg.reshape(B, T, HGRN_HEADS, HGRN_DV).astype(f32))
            hs.append(hS)
            mixed = jnp.concatenate([fo.reshape(B, T, FOX_W), ho.reshape(B, T, HGRN_VW).astype(x.dtype)], axis=-1)
            x = x + mixed @ w_out_a[j]
        else:
            z, xbc, dtr = split_cols(xn @ w_in_c[j], (SSD_INNER, SSD_CONV_DIM, SSD_HEADS))
            xpad = jnp.concatenate([conv0[j].astype(xbc.dtype), xbc], axis=1)
            cs.append(xpad[:, -(SSD_CONV - 1):])
            xbc = lax.conv_general_dilated(xpad, conv_w[j].astype(xpad.dtype)[:, None, :], (1,), 'VALID',
                                           dimension_numbers=('NWC', 'WIO', 'NWC'),
                                           feature_group_count=SSD_CONV_DIM)
            xbc = jax.nn.silu(xbc + conv_b[j])
            xs, bm, cm = split_cols(xbc, (SSD_INNER, SSD_GROUPS * SSD_N, SSD_GROUPS * SSD_N))
            dt = jax.nn.softplus(dtr.astype(f32) + dt_bias[j].astype(f32))
            a_neg = -jnp.exp(a_log[j].astype(f32)).reshape(SSD_GROUPS, SSD_R)
            xs5 = xs.reshape(B, T, SSD_GROUPS, SSD_R, SSD_HD)
            y, S = ssd_chunked(xs5, dt.reshape(B, T, SSD_GROUPS, SSD_R),
                               bm.reshape(B, T, SSD_GROUPS, SSD_N), cm.reshape(B, T, SSD_GROUPS, SSD_N),
                               a_neg, ssm0[j].reshape(-1, SSD_GROUPS, SSD_R, SSD_HD, SSD_N))
            y = y + d_skip[j].astype(f32).reshape(SSD_GROUPS, SSD_R)[..., None] * xs5.astype(f32)
            ss.append(S.reshape(B, SSD_HEADS, SSD_HD, SSD_N))
            yg = y.reshape(B, T, SSD_INNER) * jax.nn.silu(z.astype(f32))
            yg = rmsnorm(yg.reshape(B, T, SSD_GROUPS, SSD_INNER // SSD_GROUPS),
                         ssd_norm_w[j].reshape(SSD_GROUPS, SSD_INNER // SSD_GROUPS))
            x = x + yg.reshape(B, T, SSD_INNER).astype(x.dtype) @ w_out_c[j]
        xn = rmsnorm(x, norm_ffn[layer])
        x = x + peer_ffn(xn.reshape(B * T, D), peer_wq[layer], peer_keys[layer],
                         peer_u[layer], peer_v[layer]).reshape(B, T, D)
    y = rmsnorm(x, norm_final)
    return (y, jnp.stack(fks), jnp.stack(fvs), jnp.stack(fls), jnp.stack(hs), jnp.stack(ss), jnp.stack(cs))


def setup_inputs(seed: int = 0) -> dict:
    key = jax.random.key(seed)
    ks = iter(jax.random.split(key, 48))
    nrm = lambda shape, scale: scale * jax.random.normal(next(ks), shape, f32)
    n_pages = PAST_LEN // PAGE_SIZE
    n_used = DEC_BATCH * n_pages
    n_phys = (5 * n_used + 3) // 4
    page_table = jax.random.permutation(next(ks), n_phys)[:n_used].reshape(DEC_BATCH, n_pages).astype(jnp.int32)
    dt0 = jnp.exp(jax.random.uniform(next(ks), (N_ODD, SSD_HEADS), f32) * (math.log(0.1) - math.log(0.001)) + math.log(0.001))
    return {
        'x_prompt': nrm((BATCH, SEQ, D_MODEL), 1.0),
        'x_sample': nrm((DEC_BATCH, DEC_SEQ, D_MODEL), 1.0),
        'cache_fox_k': nrm((N_EVEN, n_phys, PAGE_SIZE, FOX_HEADS, FOX_HD), 1.0),
        'cache_fox_v': nrm((N_EVEN, n_phys, PAGE_SIZE, FOX_HEADS, FOX_HD), 1.0),
        'cache_fox_logf': jax.nn.log_sigmoid(nrm((N_EVEN, n_phys, PAGE_SIZE, FOX_HEADS), 1.0) + 2.0),
        'state_hgrn': nrm((N_EVEN, DEC_BATCH, HGRN_HEADS, HGRN_DK, HGRN_DV), 0.1),
        'state_ssm': nrm((N_ODD, DEC_BATCH, SSD_HEADS, SSD_HD, SSD_N), 0.1),
        'state_conv': nrm((N_ODD, DEC_BATCH, SSD_CONV - 1, SSD_CONV_DIM), 1.0),
        'page_table': page_table,
        'norm_mix': 1.0 + nrm((DEPTH, D_MODEL), 0.02),
        'norm_ffn': 1.0 + nrm((DEPTH, D_MODEL), 0.02),
        'norm_final': 1.0 + nrm((D_MODEL,), 0.02),
        'w_in_a': nrm((N_EVEN, D_MODEL, IN_A), D_MODEL ** -0.5),
        'b_fox_f': nrm((N_EVEN, FOX_HEADS), 0.1),
        'hgrn_lb_logits': nrm((N_EVEN + 1, HGRN_KW), 0.1),
        'hgrn_norm_w': 1.0 + nrm((N_EVEN, HGRN_DV), 0.02),
        'w_out_a': nrm((N_EVEN, MIX_A, D_MODEL), MIX_A ** -0.5),
        'w_in_c': nrm((N_ODD, D_MODEL, IN_C), D_MODEL ** -0.5),
        'conv_w': nrm((N_ODD, SSD_CONV, SSD_CONV_DIM), SSD_CONV ** -0.5),
        'conv_b': nrm((N_ODD, SSD_CONV_DIM), 0.02),
        'dt_bias': dt0 + jnp.log(-jnp.expm1(-dt0)),
        'a_log': jnp.log(jax.random.uniform(next(ks), (N_ODD, SSD_HEADS), f32, 1.0, 16.0)),
        'd_skip': 1.0 + nrm((N_ODD, SSD_HEADS), 0.1),
        'ssd_norm_w': 1.0 + nrm((N_ODD, SSD_INNER), 0.02),
        'w_out_c': nrm((N_ODD, SSD_INNER, D_MODEL), SSD_INNER ** -0.5),
        'peer_wq': nrm((DEPTH, D_MODEL, PEER_HEADS * PEER_DQ), D_MODEL ** -0.5),
        'peer_keys': nrm((DEPTH, PEER_HEADS, 2, PEER_NKEYS, PEER_DQ // 2), (PEER_DQ // 2) ** -0.5),
        'peer_u': nrm((DEPTH, PEER_EXPERTS, D_MODEL), D_MODEL ** -0.5),
        'peer_v': nrm((DEPTH, PEER_EXPERTS, D_MODEL), 0.25),
    }


def reference(x_prompt, x_sample, cache_fox_k, cache_fox_v, cache_fox_logf, state_hgrn, state_ssm, state_conv,
              page_table, norm_mix, norm_ffn, norm_final, w_in_a, b_fox_f, hgrn_lb_logits, hgrn_norm_w, w_out_a,
              w_in_c, conv_w, conv_b, dt_bias, a_log, d_skip, ssd_norm_w, w_out_c,
              peer_wq, peer_keys, peer_u, peer_v):
    params = (norm_mix, norm_ffn, norm_final, w_in_a, b_fox_f, hgrn_lb_logits, hgrn_norm_w, w_out_a,
              w_in_c, conv_w, conv_b, dt_bias, a_log, d_skip, ssd_norm_w, w_out_c,
              peer_wq, peer_keys, peer_u, peer_v)
    Bp = x_prompt.shape[0]
    hgrn0_p = jnp.zeros((N_EVEN, Bp, HGRN_HEADS, HGRN_DK, HGRN_DV), f32)
    ssm0_p = jnp.zeros((N_ODD, Bp, SSD_HEADS, SSD_HD, SSD_N), f32)
    conv0_p = jnp.zeros((N_ODD, Bp, SSD_CONV - 1, SSD_CONV_DIM), x_prompt.dtype)
    y_p, fk_p, fv_p, fl_p, h_p, s_p, c_p = trunk(x_prompt, None, hgrn0_p, ssm0_p, conv0_p, params)
    y_s, fk_s, fv_s, fl_s, h_s, s_s, c_s = trunk(
        x_sample, (cache_fox_k, cache_fox_v, cache_fox_logf, page_table), state_hgrn, state_ssm, state_conv, params)
    return (y_p, y_s, fk_p, fv_p, fl_p, fk_s, fv_s, fl_s, h_p, h_s, s_p, s_s, c_p, c_s)
```

```python
import functools
import math

import jax
import jax.numpy as jnp
from jax import lax
from jax.experimental import pallas as pl
from jax.experimental.pallas import tpu as pltpu

f32 = jnp.float32
bf16 = jnp.bfloat16
i32 = jnp.int32

EPS = 1e-6
LANES = 128
VMEM_LIMIT = 48 * 1024 * 1024

FOX_HD = 128
FOX_HEADS = 8
FOX_W = FOX_HEADS * FOX_HD
HGRN_HEADS = 8
HGRN_DK = 128
HGRN_DV = 128
HGRN_W = HGRN_HEADS * HGRN_DK
SSD_HD = 64
SSD_HEADS = 64
SSD_GROUPS = 8
SSD_R = SSD_HEADS // SSD_GROUPS
SSD_N = 128
SSD_INNER = SSD_HEADS * SSD_HD
SSD_GW = SSD_INNER // SSD_GROUPS
SSD_CONV = 4
SSD_CONV_DIM = SSD_INNER + 2 * SSD_GROUPS * SSD_N
PEER_HEADS = 8
PEER_NKEYS = 128
PEER_DQ = 256
PEER_TOPK = 16
PEER_SLOTS = PEER_HEADS * PEER_TOPK
PAGE = 128
SEQ_CHUNK = 128
DIAG = 16
N_TAIL = 512

_NT = (((1,), (1,)), ((), ()))
_TN = (((0,), (0,)), ((), ()))


def _params(*sem):
    return pltpu.CompilerParams(dimension_semantics=sem, vmem_limit_bytes=VMEM_LIMIT)


def _tile(n, target):
    t = min(n, target)
    while n % t:
        t -= 8
    return t


def _cumsum_axis(x, axis):
    n = x.shape[axis]
    idx = lax.broadcasted_iota(i32, x.shape, axis)
    s = 1
    while s < n:
        x = x + jnp.where(idx >= s, pltpu.roll(x, s, axis=axis), 0.0)
        s *= 2
    return x


def _rms(x, w):
    ms = jnp.mean(x * x, axis=-1, keepdims=True)
    return x * lax.rsqrt(ms + EPS) * w


def _norm_matmul_body(x_ref, nw_ref, w_ref, o_ref, xn_ref):
    @pl.when(pl.program_id(1) == 0)
    def _():
        xn_ref[...] = _rms(x_ref[...], nw_ref[...]).astype(bf16)

    o_ref[...] = jnp.dot(xn_ref[...], w_ref[...], preferred_element_type=f32)


def norm_matmul(x, nw, w, name):
    M, K = x.shape
    N = w.shape[1]
    tm, tn = _tile(M, 512), _tile(N, 512)
    return pl.pallas_call(
        _norm_matmul_body,
        grid=(M // tm, N // tn),
        in_specs=[pl.BlockSpec((tm, K), lambda i, j: (i, 0)),
                  pl.BlockSpec((1, K), lambda i, j: (0, 0)),
                  pl.BlockSpec((K, tn), lambda i, j: (0, j))],
        out_specs=pl.BlockSpec((tm, tn), lambda i, j: (i, j)),
        out_shape=jax.ShapeDtypeStruct((M, N), f32),
        scratch_shapes=[pltpu.VMEM((tm, K), bf16)],
        compiler_params=_params("parallel", "arbitrary"),
        name=name,
    )(x, nw.reshape(1, K), w)


def _matmul_res_body(a_ref, w_ref, r_ref, o_ref):
    o_ref[...] = r_ref[...] + jnp.dot(a_ref[...].astype(bf16), w_ref[...], preferred_element_type=f32)


def matmul_residual(a, w, res, name):
    M, K = a.shape
    N = w.shape[1]
    tm, tn = _tile(M, 512), _tile(N, 512)
    return pl.pallas_call(
        _matmul_res_body,
        grid=(M // tm, N // tn),
        in_specs=[pl.BlockSpec((tm, K), lambda i, j: (i, 0)),
                  pl.BlockSpec((K, tn), lambda i, j: (0, j)),
                  pl.BlockSpec((tm, tn), lambda i, j: (i, j))],
        out_specs=pl.BlockSpec((tm, tn), lambda i, j: (i, j)),
        out_shape=jax.ShapeDtypeStruct((M, N), f32),
        compiler_params=_params("parallel", "arbitrary"),
        name=name,
    )(a, w, res)


def _final_norm_body(x_ref, nw_ref, o_ref):
    o_ref[...] = _rms(x_ref[...], nw_ref[...])


def final_norm(x, nw):
    M, K = x.shape
    tm = _tile(M, 512)
    return pl.pallas_call(
        _final_norm_body,
        grid=(M // tm,),
        in_specs=[pl.BlockSpec((tm, K), lambda i: (i, 0)), pl.BlockSpec((1, K), lambda i: (0, 0))],
        out_specs=pl.BlockSpec((tm, K), lambda i: (i, 0)),
        out_shape=jax.ShapeDtypeStruct((M, K), f32),
        compiler_params=_params("parallel"),
        name="final_norm",
    )(x, nw.reshape(1, K))


def _fox_gate_body(ff_ref, b_ref, lf_ref, c_ref):
    lf = jax.nn.log_sigmoid(ff_ref[...] + b_ref[...])
    lf_ref[...] = lf
    c_ref[...] = _cumsum_axis(lf, 1)


def fox_gate(ff_t, bias_col):
    R, T = ff_t.shape
    shp = jax.ShapeDtypeStruct((R, T), f32)
    return pl.pallas_call(_fox_gate_body, out_shape=[shp, shp], name="fox_gate",
                          compiler_params=pltpu.CompilerParams(vmem_limit_bytes=VMEM_LIMIT))(ff_t, bias_col)


def _fox_attn_body(q_ref, k_ref, v_ref, cq_ref, ck_ref, o_ref, m_sc, l_sc, acc_sc, cc_sc, *, tq, tk, scale):
    h = pl.program_id(1)
    qi = pl.program_id(2)
    ki = pl.program_id(3)

    @pl.when(ki == 0)
    def _():
        m_sc[...] = jnp.full(m_sc.shape, -jnp.inf, f32)
        l_sc[...] = jnp.zeros(l_sc.shape, f32)
        acc_sc[...] = jnp.zeros(acc_sc.shape, f32)
        cq = cq_ref[0]
        lane = lax.broadcasted_iota(i32, cq.shape, 1)
        cc_sc[...] = jnp.sum(jnp.where(lane == h, cq, 0.0), axis=1, keepdims=True)

    @pl.when(ki * tk <= qi * tq + (tq - 1))
    def _():
        q = q_ref[...].astype(bf16)
        k = k_ref[...].astype(bf16)
        s = lax.dot_general(q, k, _NT, preferred_element_type=f32) * scale
        s = s + cc_sc[...] - ck_ref[0]
        qpos = qi * tq + lax.broadcasted_iota(i32, s.shape, 0)
        kpos = ki * tk + lax.broadcasted_iota(i32, s.shape, 1)
        s = jnp.where(kpos <= qpos, s, -jnp.inf)
        m_prev = m_sc[...]
        m_new = jnp.maximum(m_prev, jnp.max(s, axis=1, keepdims=True))
        alpha = jnp.exp(m_prev - m_new)
        p = jnp.exp(s - m_new)
        l_sc[...] = alpha * l_sc[...] + jnp.sum(p, axis=1, keepdims=True)
        acc_sc[...] = alpha * acc_sc[...] + jnp.dot(p.astype(bf16), v_ref[...].astype(bf16),
                                                    preferred_element_type=f32)
        m_sc[...] = m_new

    @pl.when(ki == pl.num_programs(3) - 1)
    def _():
        o_ref[...] = acc_sc[...] / l_sc[...]


def fox_prompt_attn(proj, c_bth, c_rows, B, T):
    tq = tk = _tile(T, 256)
    nq, nk = T // tq, T // tk
    H = FOX_HEADS
    body = functools.partial(_fox_attn_body, tq=tq, tk=tk, scale=FOX_HD ** -0.5)
    return pl.pallas_call(
        body,
        grid=(B, H, nq, nk),
        in_specs=[pl.BlockSpec((tq, FOX_HD), lambda b, h, qi, ki: (b * nq + qi, h)),
                  pl.BlockSpec((tk, FOX_HD), lambda b, h, qi, ki: (b * nk + jnp.minimum(ki, qi), H + h)),
                  pl.BlockSpec((tk, FOX_HD), lambda b, h, qi, ki: (b * nk + jnp.minimum(ki, qi), 2 * H + h)),
                  pl.BlockSpec((1, tq, H), lambda b, h, qi, ki: (b, qi, 0)),
                  pl.BlockSpec((1, 1, tk), lambda b, h, qi, ki: (b * H + h, 0, jnp.minimum(ki, qi)))],
        out_specs=pl.BlockSpec((tq, FOX_HD), lambda b, h, qi, ki: (b * nq + qi, h)),
        out_shape=jax.ShapeDtypeStruct((B * T, FOX_W), f32),
        scratch_shapes=[pltpu.VMEM((tq, 1), f32), pltpu.VMEM((tq, 1), f32), pltpu.VMEM((tq, FOX_HD), f32),
                        pltpu.VMEM((tq, 1), f32)],
        compiler_params=_params("parallel", "parallel", "parallel", "arbitrary"),
        name="fox_prompt_attn",
    )(proj, proj, proj, c_bth, c_rows)


def _suffix_body(x_ref, o_ref):
    x = x_ref[...]
    lane = lax.broadcasted_iota(i32, x.shape, 1)
    n = x.shape[1]
    y = jnp.where(lane < n - 1, pltpu.roll(x, n - 1, axis=1), 0.0)
    s = 1
    while s < n:
        y = y + jnp.where(lane < n - s, pltpu.roll(y, n - s, axis=1), 0.0)
        s *= 2
    o_ref[...] = y


def page_suffix_sums(lf_rows):
    R, n = lf_rows.shape
    tr = _tile(R, 2048)
    return pl.pallas_call(
        _suffix_body,
        grid=(R // tr,),
        in_specs=[pl.BlockSpec((tr, n), lambda i: (i, 0))],
        out_specs=pl.BlockSpec((tr, n), lambda i: (i, 0)),
        out_shape=jax.ShapeDtypeStruct((R, n), f32),
        compiler_params=_params("parallel"),
        name="fox_page_suffix",
    )(lf_rows)


def _fox_dec_body(pt_ref, q_ref, kn_ref, vn_ref, cn_ref, kp_ref, vp_ref, lfp_ref, sfx_ref, o_ref,
                  m_sc, l_sc, acc_sc, carry_sc, *, scale, n_q):
    del pt_ref
    H = FOX_HEADS
    j = pl.program_id(1)
    q = q_ref[0]
    cn = cn_ref[0]
    cn_rows = jnp.concatenate([cn[:, i:i + 1] for i in range(n_q)], axis=0)

    def step(k, v, bias, mask):
        s = lax.dot_general(q, k.astype(bf16), _NT, preferred_element_type=f32) * scale + bias
        if mask is not None:
            s = jnp.where(mask, s, -jnp.inf)
        m_prev = m_sc[...]
        m_new = jnp.maximum(m_prev, jnp.max(s, axis=1, keepdims=True))
        alpha = jnp.exp(m_prev - m_new)
        p = jnp.exp(s - m_new)
        l_sc[...] = alpha * l_sc[...] + jnp.sum(p, axis=1, keepdims=True)
        acc_sc[...] = alpha * acc_sc[...] + jnp.dot(p.astype(bf16), v.astype(bf16), preferred_element_type=f32)
        m_sc[...] = m_new

    @pl.when(j == 0)
    def _():
        m_sc[...] = jnp.full(m_sc.shape, -jnp.inf, f32)
        l_sc[...] = jnp.zeros(l_sc.shape, f32)
        acc_sc[...] = jnp.zeros(acc_sc.shape, f32)
        carry_sc[...] = jnp.zeros(carry_sc.shape, f32)
        bias = cn_rows - jnp.concatenate([cn] * n_q, axis=0)
        row = lax.broadcasted_iota(i32, bias.shape, 0)
        col = lax.broadcasted_iota(i32, bias.shape, 1)
        step(kn_ref[0], vn_ref[0], bias, col <= row // H)

    @pl.when(j > 0)
    def _():
        carry = carry_sc[...]
        sfx = sfx_ref[0]
        b8 = carry + sfx
        bias = cn_rows + jnp.concatenate([b8] * n_q, axis=0)
        step(kp_ref[0], vp_ref[0], bias, None)
        carry_sc[...] = carry + sfx[:, 0:1] + lfp_ref[0][:, 0:1]

    @pl.when(j == pl.num_programs(1) - 1)
    def _():
        acc = acc_sc[...] / l_sc[...]
        hrow = lax.broadcasted_iota(i32, (H, FOX_W), 0)
        hcol = lax.broadcasted_iota(i32, (H, FOX_W), 1) // FOX_HD
        for qi in range(n_q):
            a = acc[qi * H:(qi + 1) * H, :]
            o_ref[0, pl.ds(qi, 1), :] = jnp.sum(jnp.where(hrow == hcol, a, 0.0), axis=0, keepdims=True)


def fox_decode_attn(q_bd, k_new, v_new, c_new, k_pool, v_pool, lf_pool_t, sfx_pool_t, page_table, n_q):
    Bd, n_pages = page_table.shape
    H = FOX_HEADS
    R = n_q * H

    def page(b, j, pt):
        return (pt[b, n_pages - jnp.maximum(j, 1)], 0, 0)

    same = lambda b, j, pt: (b, 0, 0)
    grid_spec = pltpu.PrefetchScalarGridSpec(
        num_scalar_prefetch=1,
        grid=(Bd, n_pages + 1),
        in_specs=[pl.BlockSpec((1, R, FOX_W), same),
                  pl.BlockSpec((1, PAGE, FOX_W), same),
                  pl.BlockSpec((1, PAGE, FOX_W), same),
                  pl.BlockSpec((1, H, LANES), same),
                  pl.BlockSpec((1, PAGE, FOX_W), page),
                  pl.BlockSpec((1, PAGE, FOX_W), page),
                  pl.BlockSpec((1, H, PAGE), page),
                  pl.BlockSpec((1, H, PAGE), page)],
        out_specs=pl.BlockSpec((1, n_q, FOX_W), same),
        scratch_shapes=[pltpu.VMEM((R, 1), f32), pltpu.VMEM((R, 1), f32), pltpu.VMEM((R, FOX_W), f32),
                        pltpu.VMEM((H, 1), f32)],
    )
    body = functools.partial(_fox_dec_body, scale=FOX_HD ** -0.5, n_q=n_q)
    return pl.pallas_call(
        body,
        grid_spec=grid_spec,
        out_shape=jax.ShapeDtypeStruct((Bd, n_q, FOX_W), f32),
        compiler_params=_params("parallel", "arbitrary"),
        name="fox_decode_attn",
    )(page_table, q_bd, k_new, v_new, c_new, k_pool, v_pool, lf_pool_t, sfx_pool_t)


def _gla_body(q_ref, f_ref, i_ref, g_ref, lbl_ref, nw_ref, s0_ref, o_ref, so_ref,
              s_sc, q_sc, k_sc, b_sc, v_sc, oa_sc, *, C, t_valid, layer_j):
    c = pl.program_id(2)

    @pl.when(c == 0)
    def _():
        s_sc[...] = s0_ref[0, 0]

    lg = lbl_ref[...]
    e = jnp.exp(lg - jnp.max(lg, axis=0, keepdims=True))
    sm = e / jnp.sum(e, axis=0, keepdims=True)
    lb = jnp.sum(sm[0:layer_j + 1], axis=0, keepdims=True)

    z = f_ref[...]
    hlogf = jnp.log(lb + (1.0 - lb) * jax.nn.sigmoid(z))
    hk = (1.0 - lb) * jax.nn.sigmoid(-z)
    valid = (c * C + lax.broadcasted_iota(i32, (C, 1), 0)) < t_valid
    lc = jnp.where(valid, hlogf, 0.0)
    k = jnp.where(valid, hk, 0.0)
    b = _cumsum_axis(lc, 0)
    q = q_ref[...]
    v = i_ref[...]
    s_old = s_sc[...]

    q_sc[...] = q
    k_sc[...] = k
    b_sc[...] = b
    v_sc[...] = v
    oa_sc[...] = jnp.dot((q * jnp.exp(b)).astype(bf16), s_old.astype(bf16), preferred_element_type=f32)

    def off_diag(lo, hi):
        if hi - lo <= DIAG:
            return
        mid = (lo + hi) // 2
        r = b_sc[pl.ds(mid - 1, 1), :]
        qh = q_sc[mid:hi, :] * jnp.exp(b_sc[mid:hi, :] - r)
        kh = k_sc[lo:mid, :] * jnp.exp(r - b_sc[lo:mid, :])
        a = lax.dot_general(qh.astype(bf16), kh.astype(bf16), _NT, preferred_element_type=f32)
        oa_sc[mid:hi, :] += jnp.dot(a.astype(bf16), v_sc[lo:mid, :].astype(bf16), preferred_element_type=f32)
        off_diag(lo, mid)
        off_diag(mid, hi)

    off_diag(0, C)

    srow = lax.broadcasted_iota(i32, (DIAG, 1), 0)

    def diag(i, carry):
        lo = pl.multiple_of(i * DIAG, DIAG)
        qb = q_sc[pl.ds(lo, DIAG), :]
        kb = k_sc[pl.ds(lo, DIAG), :]
        bb = b_sc[pl.ds(lo, DIAG), :]
        vb = v_sc[pl.ds(lo, DIAG), :]
        ob = jnp.zeros((DIAG, vb.shape[1]), f32)
        for t in range(DIAG):
            dec = jnp.exp(jnp.where(srow <= t, bb[t:t + 1, :] - bb, -jnp.inf))
            w = jnp.sum(qb[t:t + 1, :] * kb * dec, axis=1, keepdims=True)
            ot = jnp.sum(w * vb, axis=0, keepdims=True)
            ob = jnp.where(srow == t, ot, ob)
        oa_sc[pl.ds(lo, DIAG), :] += ob
        return carry

    lax.fori_loop(0, C // DIAG, diag, 0)

    bl = b[C - 1:C, :]
    bl_col = b.T[:, C - 1:C]
    kd = k * jnp.exp(bl - b)
    s_new = s_old * jnp.exp(bl_col) + lax.dot_general(kd.astype(bf16), v.astype(bf16), _TN,
                                                      preferred_element_type=f32)
    s_sc[...] = s_new

    o_ref[...] = _rms(oa_sc[...], nw_ref[...]) * jax.nn.silu(g_ref[...])

    @pl.when(c == pl.num_programs(2) - 1)
    def _():
        so_ref[0, 0] = s_new


def gla(src, cb0, lb_logits, norm_w, s0, B, Tp, t_valid, layer_j):
    H, C = HGRN_HEADS, SEQ_CHUNK
    nc = Tp // C
    L = lb_logits.shape[0]

    def col(g):
        return pl.BlockSpec((C, HGRN_DK), lambda b, h, c: (b * nc + c, cb0 + g * H + h))

    body = functools.partial(_gla_body, C=C, t_valid=t_valid, layer_j=layer_j)
    return pl.pallas_call(
        body,
        grid=(B, H, nc),
        in_specs=[col(0), col(1), col(2), col(3),
                  pl.BlockSpec((L, HGRN_DK), lambda b, h, c: (0, h)),
                  pl.BlockSpec((1, HGRN_DV), lambda b, h, c: (0, 0)),
                  pl.BlockSpec((1, 1, HGRN_DK, HGRN_DV), lambda b, h, c: (b, h, 0, 0))],
        out_specs=[pl.BlockSpec((C, HGRN_DV), lambda b, h, c: (b * nc + c, h)),
                   pl.BlockSpec((1, 1, HGRN_DK, HGRN_DV), lambda b, h, c: (b, h, 0, 0))],
        out_shape=[jax.ShapeDtypeStruct((B * Tp, HGRN_W), f32),
                   jax.ShapeDtypeStruct((B, H, HGRN_DK, HGRN_DV), f32)],
        scratch_shapes=[pltpu.VMEM((HGRN_DK, HGRN_DV), f32)] + [pltpu.VMEM((C, HGRN_DK), f32)] * 5,
        compiler_params=_params("parallel", "parallel", "arbitrary"),
        name="hgrn2_gla",
    )(src, src, src, src, lb_logits, norm_w.reshape(1, HGRN_DV), s0)


def _conv_body(x_ref, w_ref, b_ref, c0_ref, o_ref, xp_sc, *, tc):
    t = pl.program_id(2)

    @pl.when(t == 0)
    def _():
        xp_sc[0:8, :] = c0_ref[0]

    xp_sc[8:8 + tc, :] = x_ref[...]
    w = w_ref[...]
    acc = b_ref[...] + w[0:1, :] * xp_sc[pl.ds(5, tc), :]
    for i in range(1, SSD_CONV):
        acc = acc + w[i:i + 1, :] * xp_sc[pl.ds(5 + i, tc), :]
    o_ref[...] = jax.nn.silu(acc)
    xp_sc[0:8, :] = xp_sc[tc:tc + 8, :]


def conv_silu(src, cb0, conv_w, conv_b, conv0_p, B, Tp):
    W = SSD_CONV_DIM
    wb = 512
    tc = _tile(Tp, 512)
    nt = Tp // tc
    body = functools.partial(_conv_body, tc=tc)
    return pl.pallas_call(
        body,
        grid=(B, W // wb, nt),
        in_specs=[pl.BlockSpec((tc, wb), lambda b, w, t: (b * nt + t, cb0 + w)),
                  pl.BlockSpec((SSD_CONV, wb), lambda b, w, t: (0, w)),
                  pl.BlockSpec((1, wb), lambda b, w, t: (0, w)),
                  pl.BlockSpec((1, 8, wb), lambda b, w, t: (b, 0, w))],
        out_specs=pl.BlockSpec((tc, wb), lambda b, w, t: (b * nt + t, w)),
        out_shape=jax.ShapeDtypeStruct((B * Tp, W), f32),
        scratch_shapes=[pltpu.VMEM((tc + 8, wb), f32)],
        compiler_params=_params("parallel", "parallel", "arbitrary"),
        name="ssd_conv",
    )(src, conv_w, conv_b.reshape(1, W), conv0_p)


def _ssd_body(z_ref, x_ref, bm_ref, cm_ref, dt_ref, dtb_ref, alog_ref, dsk_ref, nw_ref, s0_ref,
              y_ref, so_ref, st_sc, *, C, t_valid):
    g = pl.program_id(1)
    c = pl.program_id(2)
    R = SSD_R
    npair = SSD_GW // LANES

    @pl.when(c == 0)
    def _():
        s0 = s0_ref[0, 0]
        for i in range(npair):
            st_sc[:, i * LANES:(i + 1) * LANES] = s0[i * LANES:(i + 1) * LANES, :].T

    shift = (LANES - g * R) % LANES
    dt_all = jax.nn.softplus(dt_ref[...] + dtb_ref[0:1, :])
    valid = (c * C + lax.broadcasted_iota(i32, (C, 1), 0)) < t_valid
    dt_all = jnp.where(valid, dt_all, 0.0)
    la = dt_all * (-jnp.exp(alog_ref[0:1, :]))
    cs_all = _cumsum_axis(la, 0)
    cs = pltpu.roll(cs_all, shift, axis=1)
    dt = pltpu.roll(dt_all, shift, axis=1)
    dsk = pltpu.roll(dsk_ref[...], shift, axis=1)[0:1, :]
    cs_t = cs.T
    dt_t = dt.T
    cl = cs[C - 1:C, :]
    ecs = jnp.exp(cs)
    wst = dt * jnp.exp(cl - cs)
    ecl = jnp.exp(cl)

    bm = bm_ref[...].astype(bf16)
    cm = cm_ref[...].astype(bf16)
    xs = x_ref[...]
    cb = lax.dot_general(cm, bm, _NT, preferred_element_type=f32)
    causal = lax.broadcasted_iota(i32, (C, C), 0) >= lax.broadcasted_iota(i32, (C, C), 1)
    low = lax.broadcasted_iota(i32, (C, LANES), 1) < SSD_HD

    def expand(a):
        lw = low[0:a.shape[0], :]
        return jnp.concatenate(
            [jnp.where(lw, a[:, 2 * p:2 * p + 1], a[:, 2 * p + 1:2 * p + 2]) for p in range(npair)], axis=1)

    ys = []
    for p in range(npair):
        xp = xs[:, p * LANES:(p + 1) * LANES]
        acc = None
        for hh in range(2):
            r = 2 * p + hh
            dec = jnp.exp(jnp.where(causal, cs[:, r:r + 1] - cs_t[r:r + 1, :], -jnp.inf))
            wm = (cb * dec * dt_t[r:r + 1, :]).astype(bf16)
            xm = jnp.where(low if hh == 0 else jnp.logical_not(low), xp, 0.0).astype(bf16)
            part = jnp.dot(wm, xm, preferred_element_type=f32)
            acc = part if acc is None else acc + part
        ys.append(acc)
    y = jnp.concatenate(ys, axis=1)

    st = st_sc[...]
    y = y + jnp.dot(cm, st.astype(bf16), preferred_element_type=f32) * expand(ecs)
    y = y + expand(dsk) * xs

    xw = (xs * expand(wst)).astype(bf16)
    st_new = st * expand(ecl) + lax.dot_general(bm, xw, _TN, preferred_element_type=f32)
    st_sc[...] = st_new

    yg = y * jax.nn.silu(z_ref[...])
    y_ref[...] = _rms(yg, nw_ref[...]).astype(y_ref.dtype)

    @pl.when(c == pl.num_programs(2) - 1)
    def _():
        for i in range(npair):
            so_ref[0, 0, i * LANES:(i + 1) * LANES, :] = st_new[:, i * LANES:(i + 1) * LANES].T


def ssd(src, xc, dt_bias8, a_log8, d_skip8, norm_w, s0, B, Tp, t_valid):
    G, C = SSD_GROUPS, SEQ_CHUNK
    nc = Tp // C
    GW = SSD_GW
    dt_blk = (SSD_INNER + SSD_CONV_DIM) // LANES
    xoff = SSD_INNER // LANES
    body = functools.partial(_ssd_body, C=C, t_valid=t_valid)
    par = lambda b, g, c: (0, 0)
    return pl.pallas_call(
        body,
        grid=(B, G, nc),
        in_specs=[pl.BlockSpec((C, GW), lambda b, g, c: (b * nc + c, g)),
                  pl.BlockSpec((C, GW), lambda b, g, c: (b * nc + c, g)),
                  pl.BlockSpec((C, SSD_N), lambda b, g, c: (b * nc + c, xoff + g)),
                  pl.BlockSpec((C, SSD_N), lambda b, g, c: (b * nc + c, xoff + G + g)),
                  pl.BlockSpec((C, LANES), lambda b, g, c: (b * nc + c, dt_blk)),
                  pl.BlockSpec((8, LANES), par), pl.BlockSpec((8, LANES), par), pl.BlockSpec((8, LANES), par),
                  pl.BlockSpec((1, GW), lambda b, g, c: (0, g)),
                  pl.BlockSpec((1, 1, GW, SSD_N), lambda b, g, c: (b, g, 0, 0))],
        out_specs=[pl.BlockSpec((C, GW), lambda b, g, c: (b * nc + c, g)),
                   pl.BlockSpec((1, 1, GW, SSD_N), lambda b, g, c: (b, g, 0, 0))],
        out_shape=[jax.ShapeDtypeStruct((B * Tp, SSD_INNER), bf16),
                   jax.ShapeDtypeStruct((B, G, GW, SSD_N), f32)],
        scratch_shapes=[pltpu.VMEM((SSD_N, GW), f32)],
        compiler_params=_params("parallel", "parallel", "arbitrary"),
        name="ssd_scan",
    )(src, xc, xc, xc, src, dt_bias8, a_log8, d_skip8, norm_w.reshape(1, SSD_INNER), s0)


def _topk_rows(s, k):
    n_rows, n = s.shape
    ridx = lax.broadcasted_iota(i32, s.shape, 0)
    kio = lax.broadcasted_iota(i32, (k, n), 0)

    def body(it, carry):
        s, vals, idxs = carry
        m = jnp.max(s, axis=0, keepdims=True)
        idx = jnp.min(jnp.where(s == m, ridx, n_rows), axis=0, keepdims=True)
        vals = jnp.where(kio == it, m, vals)
        idxs = jnp.where(kio == it, idx, idxs)
        s = jnp.where(ridx == idx, -jnp.inf, s)
        return s, vals, idxs

    _, vals, idxs = lax.fori_loop(0, k, body, (s, jnp.zeros((k, n), f32), jnp.zeros((k, n), i32)))
    return vals, idxs


def _peer_route_body(x_ref, nw_ref, wqt_ref, keys_ref, xn_ref, i1_ref, i2_ref, g_ref,
                     qt_sc, i1t_sc, i2t_sc, gt_sc, *, tm):
    K = PEER_TOPK
    xn = _rms(x_ref[...], nw_ref[...]).astype(bf16)
    xn_ref[...] = xn
    qt_sc[...] = lax.dot_general(wqt_ref[...], xn, _NT, preferred_element_type=f32)

    def head(h, carry):
        base = pl.multiple_of(h * PEER_DQ, PEER_DQ)
        rows = pl.ds(pl.multiple_of(h * K, K), K)
        for half in range(tm // LANES):
            cols = slice(half * LANES, (half + 1) * LANES)
            vals, idxs = [], []
            for part in range(2):
                qp = qt_sc[pl.ds(base + part * (PEER_DQ // 2), PEER_DQ // 2), cols].astype(bf16)
                st = jnp.dot(keys_ref[h, part], qp, preferred_element_type=f32)
                vv, ii = _topk_rows(st, K)
                vals.append(vv)
                idxs.append(ii)
            cand = jnp.concatenate([vals[0][p:p + 1, :] + vals[1] for p in range(K)], axis=0)
            top, pos = _topk_rows(cand, K)
            pp = pos // K
            qq = pos % K
            i1s = jnp.zeros((K, LANES), i32)
            i2s = jnp.zeros((K, LANES), i32)
            for r in range(K):
                i1s = jnp.where(pp == r, idxs[0][r:r + 1, :], i1s)
                i2s = jnp.where(qq == r, idxs[1][r:r + 1, :], i2s)
            e = jnp.exp(top - jnp.max(top, axis=0, keepdims=True))
            i1t_sc[rows, cols] = i1s
            i2t_sc[rows, cols] = i2s
            gt_sc[rows, cols] = e / jnp.sum(e, axis=0, keepdims=True)
        return carry

    lax.fori_loop(0, PEER_HEADS, head, 0)
    i1_ref[...] = i1t_sc[...].T
    i2_ref[...] = i2t_sc[...].T
    g_ref[...] = gt_sc[...].T


def peer_route(x, nw, wq_t, keys):
    M, D = x.shape
    tm = 256 if M % 256 == 0 else LANES
    HQ = PEER_HEADS * PEER_DQ
    S = PEER_SLOTS
    body = functools.partial(_peer_route_body, tm=tm)
    return pl.pallas_call(
        body,
        grid=(M // tm,),
        in_specs=[pl.BlockSpec((tm, D), lambda i: (i, 0)),
                  pl.BlockSpec((1, D), lambda i: (0, 0)),
                  pl.BlockSpec((HQ, D), lambda i: (0, 0)),
                  pl.BlockSpec((PEER_HEADS, 2, PEER_NKEYS, PEER_DQ // 2), lambda i: (0, 0, 0, 0))],
        out_specs=[pl.BlockSpec((tm, D), lambda i: (i, 0)),
                   pl.BlockSpec((tm, S), lambda i: (i, 0)),
                   pl.BlockSpec((tm, S), lambda i: (i, 0)),
                   pl.BlockSpec((tm, S), lambda i: (i, 0))],
        out_shape=[jax.ShapeDtypeStruct((M, D), bf16), jax.ShapeDtypeStruct((M, S), i32),
                   jax.ShapeDtypeStruct((M, S), i32), jax.ShapeDtypeStruct((M, S), f32)],
        scratch_shapes=[pltpu.VMEM((HQ, tm), f32), pltpu.VMEM((S, tm), i32), pltpu.VMEM((S, tm), i32),
                        pltpu.VMEM((S, tm), f32)],
        compiler_params=_params("parallel"),
        name="peer_route",
    )(x, nw.reshape(1, D), wq_t, keys)


def _peer_hidden_body(xn_ref, u_ref, i1_ref, i2_ref, g_ref, c_ref, h_sc, *, te):
    e = pl.program_id(1)

    @pl.when(e == 0)
    def _():
        h_sc[...] = jnp.zeros(h_sc.shape, f32)

    s = lax.dot_general(xn_ref[...], u_ref[...], _NT, preferred_element_type=f32)
    i1 = i1_ref[...]
    i2 = i2_ref[...]
    h = h_sc[...]
    nb = te // PEER_NKEYS
    for sub in range(nb):
        got = jnp.take_along_axis(s[:, sub * PEER_NKEYS:(sub + 1) * PEER_NKEYS], i2, axis=1)
        h = jnp.where(i1 == e * nb + sub, got, h)
    h_sc[...] = h

    @pl.when(e == pl.num_programs(1) - 1)
    def _():
        act = 0.5 * h * (1.0 + lax.erf(h * (2.0 ** -0.5)))
        c_ref[...] = g_ref[...] * act


def peer_hidden(xn, u, i1, i2, g):
    M, D = xn.shape
    E = u.shape[0]
    S = PEER_SLOTS
    tm, te = _tile(M, 512), 512
    tok = lambda i, e: (i, 0)
    body = functools.partial(_peer_hidden_body, te=te)
    return pl.pallas_call(
        body,
        grid=(M // tm, E // te),
        in_specs=[pl.BlockSpec((tm, D), tok),
                  pl.BlockSpec((te, D), lambda i, e: (e, 0)),
                  pl.BlockSpec((tm, S), tok), pl.BlockSpec((tm, S), tok), pl.BlockSpec((tm, S), tok)],
        out_specs=pl.BlockSpec((tm, S), tok),
        out_shape=jax.ShapeDtypeStruct((M, S), f32),
        scratch_shapes=[pltpu.VMEM((tm, S), f32)],
        compiler_params=_params("parallel", "arbitrary"),
        name="peer_hidden",
    )(xn, u, i1, i2, g)


def _peer_out_body(i1_ref, i2_ref, c_ref, v_ref, x_ref, o_ref, g3_sc, *, tm, te):
    e = pl.program_id(1)
    NK = PEER_NKEYS

    @pl.when(e == 0)
    def _():
        sub = lax.broadcasted_iota(i32, (NK, PEER_SLOTS), 0)

        def tok(n, carry):
            r1 = i1_ref[pl.ds(n, 1), :]
            r2 = i2_ref[pl.ds(n, 1), :]
            rc = c_ref[pl.ds(n, 1), :]
            at = jnp.where(sub == r1, rc, 0.0).astype(bf16)
            bt = jnp.where(sub == r2, 1.0, 0.0).astype(bf16)
            g3_sc[n] = lax.dot_general(at, bt, _NT, preferred_element_type=f32)
            return carry

        lax.fori_loop(0, tm, tok, 0)
        o_ref[...] = x_ref[...]

    nb = te // NK
    gcat = jnp.concatenate([g3_sc[:, e * nb + s, :] for s in range(nb)], axis=1).astype(bf16)
    o_ref[...] += jnp.dot(gcat, v_ref[...], preferred_element_type=f32)


def peer_out(i1, i2, c, v, x):
    M, D = x.shape
    E = v.shape[0]
    S = PEER_SLOTS
    tm, te = _tile(M, 256), 512
    tok = lambda i, e: (i, 0)
    body = functools.partial(_peer_out_body, tm=tm, te=te)
    return pl.pallas_call(
        body,
        grid=(M // tm, E // te),
        in_specs=[pl.BlockSpec((tm, S), tok), pl.BlockSpec((tm, S), tok), pl.BlockSpec((tm, S), tok),
                  pl.BlockSpec((te, D), lambda i, e: (e, 0)),
                  pl.BlockSpec((tm, D), tok)],
        out_specs=pl.BlockSpec((tm, D), tok),
        out_shape=jax.ShapeDtypeStruct((M, D), f32),
        scratch_shapes=[pltpu.VMEM((tm, PEER_NKEYS, PEER_NKEYS), f32)],
        compiler_params=_params("parallel", "arbitrary"),
        name="peer_out",
    )(i1, i2, c, v, x)


def peer_ffn(x, nw, wq_t, keys, u, v):
    M, D = x.shape
    Mp = -(-M // LANES) * LANES
    xp = jnp.pad(x, ((0, Mp - M), (0, 0))) if Mp != M else x
    xn, i1, i2, g = peer_route(xp, nw, wq_t, keys)
    c = peer_hidden(xn, u, i1, i2, g)
    out = peer_out(i1, i2, c, v, xp)
    return out[:M] if Mp != M else out


def _pad_time(a, B, T, Tp):
    if Tp == T:
        return a
    W = a.shape[1]
    return jnp.pad(a.reshape(B, T, W), ((0, 0), (0, Tp - T), (0, 0))).reshape(B * Tp, W)


def _unpad_time(a, B, T, Tp):
    if Tp == T:
        return a
    W = a.shape[1]
    return a.reshape(B, Tp, W)[:, :T].reshape(B * T, W)


def _trunk(x, past, hgrn0, ssm0, conv0, wts):
    B, T, D = x.shape
    M = B * T
    H = FOX_HEADS
    if T % SEQ_CHUNK == 0:
        Tp = T
    else:
        assert T <= SEQ_CHUNK
        Tp = SEQ_CHUNK
    x2 = x.reshape(M, D)

    proj = norm_matmul(x2, wts["norm_mix"][0], wts["w_in_a"], "in_proj_a")
    ff = proj[:, 7 * FOX_W:7 * FOX_W + H]
    Tl = -(-T // LANES) * LANES
    ff_t = ff.reshape(B, T, H).transpose(0, 2, 1).reshape(B * H, T)
    ff_t = jnp.pad(ff_t, ((0, 0), (0, Tl - T)))
    lf_t, c_t = fox_gate(ff_t, jnp.tile(wts["b_fox_f"], B).reshape(B * H, 1))
    flogf = lf_t[:, :T].reshape(B, H, T).transpose(0, 2, 1)
    fk = proj[:, FOX_W:2 * FOX_W].reshape(B, T, H, FOX_HD)
    fv = proj[:, 2 * FOX_W:3 * FOX_W].reshape(B, T, H, FOX_HD)
    if past is None:
        c_bth = c_t.reshape(B, H, T).transpose(0, 2, 1)
        fo = fox_prompt_attn(proj, c_bth, c_t.reshape(B * H, 1, T), B, T)
    else:
        k_pool, v_pool, lf_pool, page_table = past
        n_phys = k_pool.shape[0]
        lf_pool_t = lf_pool.transpose(0, 2, 1)
        sfx_pool_t = page_suffix_sums(lf_pool_t.reshape(n_phys * H, PAGE)).reshape(n_phys, H, PAGE)
        q4 = proj[:, 0:FOX_W].reshape(B, T, H, FOX_HD)
        eye = jnp.eye(H, dtype=f32)
        q_bd = (q4[:, :, :, None, :] * eye[None, None, :, :, None]).reshape(B, T * H, FOX_W).astype(bf16)
        k_new = jnp.pad(proj[:, FOX_W:2 * FOX_W].reshape(B, T, FOX_W), ((0, 0), (0, PAGE - T), (0, 0)))
        v_new = jnp.pad(proj[:, 2 * FOX_W:3 * FOX_W].reshape(B, T, FOX_W), ((0, 0), (0, PAGE - T), (0, 0)))
        fo = fox_decode_attn(q_bd, k_new, v_new, c_t.reshape(B, H, Tl),
                             k_pool.reshape(n_phys, PAGE, FOX_W), v_pool.reshape(n_phys, PAGE, FOX_W),
                             lf_pool_t, sfx_pool_t, page_table, T).reshape(M, FOX_W)

    if Tp == T:
        hsrc, hcb0 = proj, 3 * H
    else:
        hsrc, hcb0 = _pad_time(proj[:, 3 * FOX_W:3 * FOX_W + 4 * HGRN_W], B, T, Tp), 0
    ho, h_state = gla(hsrc, hcb0, wts["hgrn_lb_logits"], wts["hgrn_norm_w"], hgrn0, B, Tp, T, 0)
    ho = _unpad_time(ho, B, T, Tp)
    x2 = matmul_residual(jnp.concatenate([fo, ho], axis=1), wts["w_out_a"], x2, "out_proj_a")
    x2 = peer_ffn(x2, wts["norm_ffn"][0], wts["peer_wq_t"][0], wts["peer_keys"][0], wts["peer_u"][0],
                  wts["peer_v"][0])

    projc = norm_matmul(x2, wts["norm_mix"][1], wts["w_in_c"], "in_proj_c")
    xbc_raw = projc[:, SSD_INNER:SSD_INNER + SSD_CONV_DIM].reshape(B, T, SSD_CONV_DIM)
    if T >= SSD_CONV - 1:
        conv_state = xbc_raw[:, T - (SSD_CONV - 1):]
    else:
        conv_state = jnp.concatenate([conv0, xbc_raw], axis=1)[:, -(SSD_CONV - 1):]
    csrc = _pad_time(projc, B, T, Tp)
    conv0_p = jnp.pad(conv0, ((0, 0), (8 - (SSD_CONV - 1), 0), (0, 0)))
    xc = conv_silu(csrc, SSD_INNER // 512, wts["conv_w"], wts["conv_b"], conv0_p, B, Tp)
    yg, s_state = ssd(csrc, xc, wts["dt_bias8"], wts["a_log8"], wts["d_skip8"], wts["ssd_norm_w"],
                      ssm0.reshape(B, SSD_GROUPS, SSD_GW, SSD_N), B, Tp, T)
    x2 = matmul_residual(_unpad_time(yg, B, T, Tp), wts["w_out_c"], x2, "out_proj_c")
    x2 = peer_ffn(x2, wts["norm_ffn"][1], wts["peer_wq_t"][1], wts["peer_keys"][1], wts["peer_u"][1],
                  wts["peer_v"][1])

    y = final_norm(x2, wts["norm_final"]).reshape(B, T, D)
    return (y, fk[None], fv[None], flogf[None], h_state[None],
            s_state.reshape(B, SSD_HEADS, SSD_HD, SSD_N)[None], conv_state[None])


def _row8(a):
    return jnp.tile(jnp.pad(a.astype(f32), (0, LANES - a.shape[0]))[None, :], (8, 1))


def kernel(x_prompt, x_sample, cache_fox_k, cache_fox_v, cache_fox_logf, state_hgrn, state_ssm, state_conv, page_table, norm_mix, norm_ffn, norm_final, w_in_a, b_fox_f, hgrn_lb_logits, hgrn_norm_w, w_out_a, w_in_c, conv_w, conv_b, dt_bias, a_log, d_skip, ssd_norm_w, w_out_c, peer_wq, peer_keys, peer_u, peer_v):
    D = x_prompt.shape[-1]
    wa = w_in_a[0]
    n_main = 3 * FOX_W
    wa_r = jnp.concatenate(
        [wa[:, :n_main], wa[:, n_main + FOX_HEADS:],
         jnp.pad(wa[:, n_main:n_main + FOX_HEADS], ((0, 0), (0, N_TAIL - FOX_HEADS)))], axis=1).astype(bf16)
    wc = w_in_c[0]
    wc_r = jnp.pad(wc, ((0, 0), (0, N_TAIL - SSD_HEADS))).astype(bf16)
    wts = dict(
        norm_mix=norm_mix, norm_ffn=norm_ffn, norm_final=norm_final,
        w_in_a=wa_r, b_fox_f=b_fox_f[0], hgrn_lb_logits=hgrn_lb_logits, hgrn_norm_w=hgrn_norm_w[0],
        w_out_a=w_out_a[0].astype(bf16),
        w_in_c=wc_r, conv_w=conv_w[0], conv_b=conv_b[0],
        dt_bias8=_row8(dt_bias[0]), a_log8=_row8(a_log[0]), d_skip8=_row8(d_skip[0]),
        ssd_norm_w=ssd_norm_w[0], w_out_c=w_out_c[0].astype(bf16),
        peer_wq_t=jnp.swapaxes(peer_wq, 1, 2).astype(bf16), peer_keys=peer_keys.astype(bf16),
        peer_u=peer_u.astype(bf16), peer_v=peer_v.astype(bf16),
    )
    Bp = x_prompt.shape[0]
    hgrn0_p = jnp.zeros((Bp, HGRN_HEADS, HGRN_DK, HGRN_DV), f32)
    ssm0_p = jnp.zeros((Bp, SSD_HEADS, SSD_HD, SSD_N), f32)
    conv0_p = jnp.zeros((Bp, SSD_CONV - 1, SSD_CONV_DIM), f32)
    y_p, fk_p, fv_p, fl_p, h_p, s_p, c_p = _trunk(x_prompt, None, hgrn0_p, ssm0_p, conv0_p, wts)
    past = (cache_fox_k[0], cache_fox_v[0], cache_fox_logf[0], page_table)
    y_s, fk_s, fv_s, fl_s, h_s, s_s, c_s = _trunk(x_sample, past, state_hgrn[0], state_ssm[0], state_conv[0], wts)
    return (y_p, y_s, fk_p, fv_p, fl_p, fk_s, fv_s, fl_s, h_p, h_s, s_p, s_s, c_p, c_s)
```

```python
import functools
import math

import jax
import jax.numpy as jnp
from jax import lax
from jax.experimental import pallas as pl
from jax.experimental.pallas import tpu as pltpu

f32 = jnp.float32
bf16 = jnp.bfloat16
i32 = jnp.int32

EPS = 1e-6
LANES = 128
VMEM_LIMIT = 48 * 1024 * 1024

FOX_HD = 128
FOX_HEADS = 8
FOX_W = FOX_HEADS * FOX_HD
HGRN_HEADS = 8
HGRN_DK = 128
HGRN_DV = 128
HGRN_W = HGRN_HEADS * HGRN_DK
SSD_HD = 64
SSD_HEADS = 64
SSD_GROUPS = 8
SSD_R = SSD_HEADS // SSD_GROUPS
SSD_N = 128
SSD_INNER = SSD_HEADS * SSD_HD
SSD_GW = SSD_INNER // SSD_GROUPS
SSD_CONV = 4
SSD_CONV_DIM = SSD_INNER + 2 * SSD_GROUPS * SSD_N
PEER_HEADS = 8
PEER_NKEYS = 128
PEER_DQ = 256
PEER_TOPK = 16
PEER_SLOTS = PEER_HEADS * PEER_TOPK
PAGE = 128
SEQ_CHUNK = 128
DIAG = 16
GLA_SUB = 64
GLA_SAFE_LOG = 80.0
N_TAIL = 512

_NT = (((1,), (1,)), ((), ()))
_TN = (((0,), (0,)), ((), ()))


def _params(*sem):
    return pltpu.CompilerParams(dimension_semantics=sem, vmem_limit_bytes=VMEM_LIMIT)


def _tile(n, target):
    t = min(n, target)
    while n % t:
        t -= 8
    return t


def _cumsum_axis(x, axis, stride=1):
    n = x.shape[axis]
    idx = lax.broadcasted_iota(i32, x.shape, axis)
    s = stride
    while s < n:
        x = x + jnp.where(idx >= s, pltpu.roll(x, s, axis=axis), 0.0)
        s *= 2
    return x


def _row_reduce(x, combine, reduce):
    acc = x[:, 0:LANES]
    for i in range(1, x.shape[1] // LANES):
        acc = combine(acc, x[:, i * LANES:(i + 1) * LANES])
    return reduce(acc, axis=1, keepdims=True)


def _rms(x, w):
    ms = jnp.mean(x * x, axis=-1, keepdims=True)
    return x * lax.rsqrt(ms + EPS) * w


def _norm_matmul_body(x_ref, nw_ref, w_ref, o_ref, xn_ref):
    @pl.when(pl.program_id(1) == 0)
    def _():
        xn_ref[...] = _rms(x_ref[...], nw_ref[...]).astype(bf16)

    o_ref[...] = jnp.dot(xn_ref[...], w_ref[...], preferred_element_type=f32)


def norm_matmul(x, nw, w, name):
    M, K = x.shape
    N = w.shape[1]
    tm, tn = _tile(M, 1024), _tile(N, 768)
    return pl.pallas_call(
        _norm_matmul_body,
        grid=(M // tm, N // tn),
        in_specs=[pl.BlockSpec((tm, K), lambda i, j: (i, 0)),
                  pl.BlockSpec((1, K), lambda i, j: (0, 0)),
                  pl.BlockSpec((K, tn), lambda i, j: (0, j))],
        out_specs=pl.BlockSpec((tm, tn), lambda i, j: (i, j)),
        out_shape=jax.ShapeDtypeStruct((M, N), f32),
        scratch_shapes=[pltpu.VMEM((tm, K), bf16)],
        compiler_params=_params("parallel", "arbitrary"),
        name=name,
    )(x, nw.reshape(1, K), w)


def _matmul_res_body(a_ref, w_ref, r_ref, o_ref):
    o_ref[...] = r_ref[...] + jnp.dot(a_ref[...].astype(bf16), w_ref[...], preferred_element_type=f32)


def matmul_residual(a, w, res, name):
    M, K = a.shape
    N = w.shape[1]
    tm, tn = _tile(M, 1024), _tile(N, 512)
    return pl.pallas_call(
        _matmul_res_body,
        grid=(M // tm, N // tn),
        in_specs=[pl.BlockSpec((tm, K), lambda i, j: (i, 0)),
                  pl.BlockSpec((K, tn), lambda i, j: (0, j)),
                  pl.BlockSpec((tm, tn), lambda i, j: (i, j))],
        out_specs=pl.BlockSpec((tm, tn), lambda i, j: (i, j)),
        out_shape=jax.ShapeDtypeStruct((M, N), f32),
        compiler_params=_params("parallel", "arbitrary"),
        name=name,
    )(a, w, res)


def _final_norm_body(x_ref, nw_ref, o_ref):
    o_ref[...] = _rms(x_ref[...], nw_ref[...])


def final_norm(x, nw):
    M, K = x.shape
    tm = _tile(M, 512)
    return pl.pallas_call(
        _final_norm_body,
        grid=(M // tm,),
        in_specs=[pl.BlockSpec((tm, K), lambda i: (i, 0)), pl.BlockSpec((1, K), lambda i: (0, 0))],
        out_specs=pl.BlockSpec((tm, K), lambda i: (i, 0)),
        out_shape=jax.ShapeDtypeStruct((M, K), f32),
        compiler_params=_params("parallel"),
        name="final_norm",
    )(x, nw.reshape(1, K))


def _fox_gate_body(ff_ref, b_ref, lf_ref, c_ref, *, stride):
    lf = jax.nn.log_sigmoid(ff_ref[...] + b_ref[...])
    lf_ref[...] = lf
    c_ref[...] = _cumsum_axis(lf, 1, stride)


def fox_gate(ff, bias, stride=1):
    R, T = ff.shape
    shp = jax.ShapeDtypeStruct((R, T), f32)
    return pl.pallas_call(functools.partial(_fox_gate_body, stride=stride), out_shape=[shp, shp], name="fox_gate",
                          compiler_params=pltpu.CompilerParams(vmem_limit_bytes=VMEM_LIMIT))(ff, bias)


def _fox_attn_body(q_ref, k_ref, v_ref, ct_ref, cs_ref, o_ref, vt_sc, cs_sc, m_sc, l_sc, acc_sc, *, tq, scale, hpb):
    hg = pl.program_id(1)
    qi = pl.program_id(2)
    D = FOX_HD
    T = k_ref.shape[0]

    @pl.when(qi == 0)
    def _():
        cs = cs_ref[0]
        lane = lax.broadcasted_iota(i32, cs.shape, 1)
        for j in range(hpb):
            col = jnp.sum(jnp.where(lane == hg * hpb + j, cs, 0.0), axis=1, keepdims=True)
            cs_sc[j] = jnp.broadcast_to(col, (T, LANES))
            for t in range(T // LANES):
                blk = v_ref[t * LANES:(t + 1) * LANES, j * D:(j + 1) * D]
                vt_sc[j, :, t * LANES:(t + 1) * LANES] = blk.T.astype(bf16)

    q0 = pl.multiple_of(qi * tq, tq)
    qs, cts = [], []
    for j in range(hpb):
        qs.append(q_ref[:, j * D:(j + 1) * D].astype(bf16))
        cts.append(ct_ref[j, :, pl.ds(q0, tq)])
        m_sc[j] = jnp.full((1, tq), -jnp.inf, f32)
        l_sc[j] = jnp.zeros((1, tq), f32)
        acc_sc[j] = jnp.zeros((D, tq), f32)
    causal = lax.broadcasted_iota(i32, (tq, tq), 0) <= lax.broadcasted_iota(i32, (tq, tq), 1)

    def block(ki, masked):
        start = pl.multiple_of(ki * tq, tq)
        for j in range(hpb):
            k = k_ref[pl.ds(start, tq), j * D:(j + 1) * D].astype(bf16)
            st = lax.dot_general(k, qs[j], _NT, preferred_element_type=f32) * scale
            csb = cs_sc[j, pl.ds(start, tq), :]
            st = st + cts[j] - jnp.concatenate([csb] * (tq // LANES), axis=1)
            if masked:
                st = jnp.where(causal, st, -jnp.inf)
            m_prev = m_sc[j]
            m_new = jnp.maximum(m_prev, jnp.max(st, axis=0, keepdims=True))
            alpha = jnp.exp(m_prev - m_new)
            p = jnp.exp(st - m_new)
            l_sc[j] = alpha * l_sc[j] + jnp.sum(p, axis=0, keepdims=True)
            acc_sc[j] = alpha * acc_sc[j] + jnp.dot(vt_sc[j, :, pl.ds(start, tq)], p.astype(bf16),
                                                    preferred_element_type=f32)
            m_sc[j] = m_new

    def body(ki, carry):
        block(ki, False)
        return carry

    lax.fori_loop(0, qi, body, 0)
    block(qi, True)
    for j in range(hpb):
        o_t = acc_sc[j] / l_sc[j]
        for t in range(tq // LANES):
            o_ref[t * LANES:(t + 1) * LANES, j * D:(j + 1) * D] = o_t[:, t * LANES:(t + 1) * LANES].T


def fox_prompt_attn(proj, c_bth, c_rows, B, T):
    tq = _tile(T, 256)
    nq = T // tq
    H = FOX_HEADS
    hpb = 2
    W = hpb * FOX_HD
    G = H // hpb
    body = functools.partial(_fox_attn_body, tq=tq, scale=FOX_HD ** -0.5, hpb=hpb)
    return pl.pallas_call(
        body,
        grid=(B, G, nq),
        in_specs=[pl.BlockSpec((tq, W), lambda b, g, qi: (b * nq + qi, g)),
                  pl.BlockSpec((T, W), lambda b, g, qi: (b, G + g)),
                  pl.BlockSpec((T, W), lambda b, g, qi: (b, 2 * G + g)),
                  pl.BlockSpec((hpb, 1, T), lambda b, g, qi: (b * G + g, 0, 0)),
                  pl.BlockSpec((1, T, H), lambda b, g, qi: (b, 0, 0))],
        out_specs=pl.BlockSpec((tq, W), lambda b, g, qi: (b * nq + qi, g)),
        out_shape=jax.ShapeDtypeStruct((B * T, FOX_W), f32),
        scratch_shapes=[pltpu.VMEM((hpb, FOX_HD, T), bf16), pltpu.VMEM((hpb, T, LANES), f32),
                        pltpu.VMEM((hpb, 1, tq), f32), pltpu.VMEM((hpb, 1, tq), f32),
                        pltpu.VMEM((hpb, FOX_HD, tq), f32)],
        compiler_params=_params("parallel", "parallel", "arbitrary"),
        name="fox_prompt_attn",
    )(proj, proj, proj, c_rows, c_bth)


def _suffix_body(x_ref, sfx_ref, tot_ref, *, stride):
    x = x_ref[...]
    lane = lax.broadcasted_iota(i32, x.shape, 1)
    n = x.shape[1]
    y = jnp.where(lane < n - stride, pltpu.roll(x, n - stride, axis=1), 0.0)
    t = x
    s = stride
    while s < n:
        y = y + jnp.where(lane < n - s, pltpu.roll(y, n - s, axis=1), 0.0)
        t = t + pltpu.roll(t, s, axis=1)
        s *= 2
    sfx_ref[...] = y
    tot_ref[...] = t


def page_suffix_sums(lf_rows, stride):
    R, n = lf_rows.shape
    tr = _tile(R, 512)
    shp = jax.ShapeDtypeStruct((R, n), f32)
    return pl.pallas_call(
        functools.partial(_suffix_body, stride=stride),
        grid=(R // tr,),
        in_specs=[pl.BlockSpec((tr, n), lambda i: (i, 0))],
        out_specs=[pl.BlockSpec((tr, n), lambda i: (i, 0)), pl.BlockSpec((tr, n), lambda i: (i, 0))],
        out_shape=[shp, shp],
        compiler_params=_params("parallel"),
        name="fox_page_suffix",
    )(lf_rows)


def _fox_dec_body(pt_ref, q_ref, kn_ref, vn_ref, cn_ref, *refs, scale, G):
    del pt_ref
    kp_refs, vp_refs = refs[0:G], refs[G:2 * G]
    sfx_refs, tot_refs = refs[2 * G:3 * G], refs[3 * G:4 * G]
    o_ref, m_sc, l_sc, acc_sc, carry_sc, cc_sc = refs[4 * G:]
    H = FOX_HEADS
    j = pl.program_id(1)
    q = q_ref[0]

    def partial_softmax(k, v, bias):
        s = lax.dot_general(q, k.astype(bf16), _NT, preferred_element_type=f32) * scale + bias
        m = _row_reduce(s, jnp.maximum, jnp.max)
        p = jnp.exp(s - m)
        return m, _row_reduce(p, jnp.add, jnp.sum), jnp.dot(p.astype(bf16), v.astype(bf16),
                                                            preferred_element_type=f32)

    def merge(parts):
        m_prev = m_sc[...]
        m_new = m_prev
        for m, _, _ in parts:
            m_new = jnp.maximum(m_new, m)
        alpha = jnp.exp(m_prev - m_new)
        l = alpha * l_sc[...]
        acc = alpha * acc_sc[...]
        for m, lp, ap in parts:
            w = jnp.exp(m - m_new)
            l = l + w * lp
            acc = acc + w * ap
        m_sc[...] = m_new
        l_sc[...] = l
        acc_sc[...] = acc

    def step(k, v, bias):
        merge([partial_softmax(k, v, bias)])

    @pl.when(j == 0)
    def _():
        m_sc[...] = jnp.full(m_sc.shape, -jnp.inf, f32)
        l_sc[...] = jnp.zeros(l_sc.shape, f32)
        acc_sc[...] = jnp.zeros(acc_sc.shape, f32)
        carry_sc[...] = jnp.zeros(carry_sc.shape, f32)
        cn = cn_ref[0]
        shape = (q.shape[0], cn.shape[1])
        row = lax.broadcasted_iota(i32, shape, 0)
        col = lax.broadcasted_iota(i32, shape, 1)
        cc = jnp.sum(jnp.where(row == col, cn, 0.0), axis=1, keepdims=True)
        cc_sc[...] = cc
        ok = jnp.logical_and(col % H == row % H, col // H <= row // H)
        step(kn_ref[0], vn_ref[0], jnp.where(ok, cc - cn, -jnp.inf))

    @pl.when(j > 0)
    def _():
        carry = carry_sc[...]
        shape = (q.shape[0], carry.shape[1])
        same_head = lax.broadcasted_iota(i32, shape, 1) % H == lax.broadcasted_iota(i32, shape, 0) % H
        cc = cc_sc[...]
        parts = []
        for g in range(G):
            bias = jnp.where(same_head, cc + (carry + sfx_refs[g][0]), -jnp.inf)
            parts.append(partial_softmax(kp_refs[g][0], vp_refs[g][0], bias))
            carry = carry + tot_refs[g][0]
        merge(parts)
        carry_sc[...] = carry

    @pl.when(j == pl.num_programs(1) - 1)
    def _():
        o_ref[0] = acc_sc[...] / l_sc[...]


def fox_decode_attn(q, k_new, v_new, c_new, k_pool, v_pool, sfx_pool, tot_pool, page_table):
    Bd, n_pages = page_table.shape
    R, D = q.shape[1], q.shape[2]
    PW = k_pool.shape[1]
    G = math.gcd(n_pages, 4)

    def page(g):
        return lambda b, j, pt: (pt[b, n_pages - 1 - (jnp.maximum(j, 1) - 1) * G - g], 0, 0)

    same = lambda b, j, pt: (b, 0, 0)
    grid_spec = pltpu.PrefetchScalarGridSpec(
        num_scalar_prefetch=1,
        grid=(Bd, n_pages // G + 1),
        in_specs=([pl.BlockSpec((1, R, D), same),
                   pl.BlockSpec((1, LANES, D), same),
                   pl.BlockSpec((1, LANES, D), same),
                   pl.BlockSpec((1, 1, LANES), same)]
                  + [pl.BlockSpec((1, PW, D), page(g)) for g in range(G)]
                  + [pl.BlockSpec((1, PW, D), page(g)) for g in range(G)]
                  + [pl.BlockSpec((1, 1, PW), page(g)) for g in range(G)]
                  + [pl.BlockSpec((1, 1, PW), page(g)) for g in range(G)]),
        out_specs=pl.BlockSpec((1, R, D), same),
        scratch_shapes=[pltpu.VMEM((R, 1), f32), pltpu.VMEM((R, 1), f32), pltpu.VMEM((R, D), f32),
                        pltpu.VMEM((1, PW), f32), pltpu.VMEM((R, 1), f32)],
    )
    body = functools.partial(_fox_dec_body, scale=FOX_HD ** -0.5, G=G)
    return pl.pallas_call(
        body,
        grid_spec=grid_spec,
        out_shape=jax.ShapeDtypeStruct((Bd, R, D), f32),
        compiler_params=_params("parallel", "arbitrary"),
        name="fox_decode_attn",
    )(page_table, q, k_new, v_new, c_new, *([k_pool] * G), *([v_pool] * G), *([sfx_pool] * G),
      *([tot_pool] * G))


def _gla_body(q_ref, f_ref, i_ref, g_ref, lbl_ref, nw_ref, s0_ref, o_ref, so_ref,
              s_sc, q_sc, k_sc, b_sc, v_sc, oa_sc, *, C, t_valid, layer_j, HB):
    c = pl.program_id(2)
    SUB = GLA_SUB
    K = HGRN_DK

    @pl.when(c == 0)
    def _():
        for j in range(HB):
            s_sc[j] = s0_ref[0, j].T

    lg = lbl_ref[...]
    e = jnp.exp(lg - jnp.max(lg, axis=0, keepdims=True))
    sm = e / jnp.sum(e, axis=0, keepdims=True)
    lb = jnp.sum(sm[0:layer_j + 1], axis=0, keepdims=True)

    z = f_ref[...]
    hlogf = jnp.log(lb + (1.0 - lb) * jax.nn.sigmoid(z))
    hk = (1.0 - lb) * jax.nn.sigmoid(-z)
    valid = (c * C + lax.broadcasted_iota(i32, (C, 1), 0)) < t_valid
    lc = jnp.where(valid, hlogf, 0.0)
    kk = jnp.where(valid, hk, 0.0)
    subs = list(range(0, C, SUB))
    bs = [_cumsum_axis(lc[lo:lo + SUB, :], 0) for lo in subs]
    bl_min = bs[0][SUB - 1:SUB, :]
    for b_ in bs[1:]:
        bl_min = jnp.minimum(bl_min, b_[SUB - 1:SUB, :])
    in_range = jnp.min(bl_min) >= -GLA_SAFE_LOG
    tril = lax.broadcasted_iota(i32, (SUB, SUB), 1) <= lax.broadcasted_iota(i32, (SUB, SUB), 0)

    @pl.when(in_range)
    def _():
        for j in range(HB):
            cols = slice(j * K, (j + 1) * K)
            st = s_sc[j]
            for si, lo in enumerate(subs):
                rows = slice(lo, lo + SUB)
                b = bs[si][:, cols]
                bl = b[SUB - 1:SUB, :]
                ks = kk[rows, cols]
                vs = i_ref[rows, cols].astype(bf16)
                qd = (q_ref[rows, cols] * jnp.exp(b)).astype(bf16)
                ki = (ks * jnp.exp(-b)).astype(bf16)
                a = jnp.where(tril, lax.dot_general(qd, ki, _NT, preferred_element_type=f32), 0.0)
                o = (lax.dot_general(qd, st.astype(bf16), _NT, preferred_element_type=f32)
                     + jnp.dot(a.astype(bf16), vs, preferred_element_type=f32))
                kd = (ks * jnp.exp(bl - b)).astype(bf16)
                st = st * jnp.exp(bl) + lax.dot_general(vs, kd, _TN, preferred_element_type=f32)
                o_ref[rows, cols] = _rms(o, nw_ref[...]) * jax.nn.silu(g_ref[rows, cols])
            s_sc[j] = st

    def off_diag(lo, hi):
        if hi - lo <= DIAG:
            return
        mid = (lo + hi) // 2
        r = b_sc[pl.ds(mid - 1, 1), :]
        qh = q_sc[mid:hi, :] * jnp.exp(b_sc[mid:hi, :] - r)
        kh = k_sc[lo:mid, :] * jnp.exp(r - b_sc[lo:mid, :])
        a = lax.dot_general(qh.astype(bf16), kh.astype(bf16), _NT, preferred_element_type=f32)
        oa_sc[mid:hi, :] += jnp.dot(a.astype(bf16), v_sc[lo:mid, :].astype(bf16), preferred_element_type=f32)
        off_diag(lo, mid)
        off_diag(mid, hi)

    srow = lax.broadcasted_iota(i32, (DIAG, 1), 0)

    def diag(i, carry):
        lo = pl.multiple_of(i * DIAG, DIAG)
        qb = q_sc[pl.ds(lo, DIAG), :]
        kb = k_sc[pl.ds(lo, DIAG), :]
        bb = b_sc[pl.ds(lo, DIAG), :]
        vb = v_sc[pl.ds(lo, DIAG), :]
        ob = jnp.zeros((DIAG, vb.shape[1]), f32)
        for t in range(DIAG):
            dec = jnp.exp(jnp.where(srow <= t, bb[t:t + 1, :] - bb, -jnp.inf))
            w = jnp.sum(qb[t:t + 1, :] * kb * dec, axis=1, keepdims=True)
            ot = jnp.sum(w * vb, axis=0, keepdims=True)
            ob = jnp.where(srow == t, ot, ob)
        oa_sc[pl.ds(lo, DIAG), :] += ob
        return carry

    @pl.when(jnp.logical_not(in_range))
    def _():
        for j in range(HB):
            cols = slice(j * K, (j + 1) * K)
            q_sc[...] = q_ref[:, cols]
            k_sc[...] = kk[:, cols]
            v_sc[...] = i_ref[:, cols]
            for si, lo in enumerate(subs):
                rows = slice(lo, lo + SUB)
                b = bs[si][:, cols]
                b_sc[rows, :] = b
                bl = b[SUB - 1:SUB, :]
                st = s_sc[j]
                qd = (q_sc[rows, :] * jnp.exp(b)).astype(bf16)
                oa_sc[rows, :] = lax.dot_general(qd, st.astype(bf16), _NT, preferred_element_type=f32)
                off_diag(lo, lo + SUB)
                lax.fori_loop(lo // DIAG, (lo + SUB) // DIAG, diag, 0)
                kd = (k_sc[rows, :] * jnp.exp(bl - b)).astype(bf16)
                s_sc[j] = st * jnp.exp(bl) + lax.dot_general(v_sc[rows, :].astype(bf16), kd, _TN,
                                                             preferred_element_type=f32)
            o_ref[:, cols] = _rms(oa_sc[...], nw_ref[...]) * jax.nn.silu(g_ref[:, cols])

    @pl.when(c == pl.num_programs(2) - 1)
    def _():
        for j in range(HB):
            so_ref[0, j] = s_sc[j].T


def gla(src, cb0, lb_logits, norm_w, s0, B, Tp, t_valid, layer_j):
    H, C = HGRN_HEADS, SEQ_CHUNK
    HB = 4
    G = H // HB
    W = HB * HGRN_DK
    nc = Tp // C
    L = lb_logits.shape[0]
    assert cb0 % HB == 0

    def col(g):
        return pl.BlockSpec((C, W), lambda b, h, c: (b * nc + c, (cb0 + g * H) // HB + h))

    body = functools.partial(_gla_body, C=C, t_valid=t_valid, layer_j=layer_j, HB=HB)
    return pl.pallas_call(
        body,
        grid=(B, G, nc),
        in_specs=[col(0), col(1), col(2), col(3),
                  pl.BlockSpec((L, W), lambda b, h, c: (0, h)),
                  pl.BlockSpec((1, HGRN_DV), lambda b, h, c: (0, 0)),
                  pl.BlockSpec((1, HB, HGRN_DK, HGRN_DV), lambda b, h, c: (b, h, 0, 0))],
        out_specs=[pl.BlockSpec((C, W), lambda b, h, c: (b * nc + c, h)),
                   pl.BlockSpec((1, HB, HGRN_DK, HGRN_DV), lambda b, h, c: (b, h, 0, 0))],
        out_shape=[jax.ShapeDtypeStruct((B * Tp, HGRN_W), f32),
                   jax.ShapeDtypeStruct((B, H, HGRN_DK, HGRN_DV), f32)],
        scratch_shapes=[pltpu.VMEM((HB, HGRN_DV, HGRN_DK), f32)] + [pltpu.VMEM((C, HGRN_DK), f32)] * 5,
        compiler_params=_params("parallel", "parallel", "arbitrary"),
        name="hgrn2_gla",
    )(src, src, src, src, lb_logits, norm_w.reshape(1, HGRN_DV), s0)


def _conv_body(x_ref, w_ref, b_ref, c0_ref, o_ref, xp_sc, *, tc):
    t = pl.program_id(2)

    @pl.when(t == 0)
    def _():
        xp_sc[0:8, :] = c0_ref[0]

    xp_sc[8:8 + tc, :] = x_ref[...]
    w = w_ref[...]
    acc = b_ref[...] + w[0:1, :] * xp_sc[pl.ds(5, tc), :]
    for i in range(1, SSD_CONV):
        acc = acc + w[i:i + 1, :] * xp_sc[pl.ds(5 + i, tc), :]
    o_ref[...] = jax.nn.silu(acc)
    xp_sc[0:8, :] = xp_sc[tc:tc + 8, :]


def conv_silu(src, cb0, conv_w, conv_b, conv0_p, B, Tp):
    W = SSD_CONV_DIM
    wb = 512
    tc = _tile(Tp, 512)
    nt = Tp // tc
    body = functools.partial(_conv_body, tc=tc)
    return pl.pallas_call(
        body,
        grid=(B, W // wb, nt),
        in_specs=[pl.BlockSpec((tc, wb), lambda b, w, t: (b * nt + t, cb0 + w)),
                  pl.BlockSpec((SSD_CONV, wb), lambda b, w, t: (0, w)),
                  pl.BlockSpec((1, wb), lambda b, w, t: (0, w)),
                  pl.BlockSpec((1, 8, wb), lambda b, w, t: (b, 0, w))],
        out_specs=pl.BlockSpec((tc, wb), lambda b, w, t: (b * nt + t, w)),
        out_shape=jax.ShapeDtypeStruct((B * Tp, W), f32),
        scratch_shapes=[pltpu.VMEM((tc + 8, wb), f32)],
        compiler_params=_params("parallel", "parallel", "arbitrary"),
        name="ssd_conv",
    )(src, conv_w, conv_b.reshape(1, W), conv0_p)


def _ssd_body(z_ref, x_ref, bm_ref, cm_ref, dt_ref, dtb_ref, alog_ref, dsk_ref, nw_ref, s0_ref,
              y_ref, so_ref, st_sc, *, C, t_valid):
    g = pl.program_id(1)
    c = pl.program_id(2)
    R = SSD_R
    npair = SSD_GW // LANES

    @pl.when(c == 0)
    def _():
        s0 = s0_ref[0, 0]
        for i in range(npair):
            st_sc[:, i * LANES:(i + 1) * LANES] = s0[i * LANES:(i + 1) * LANES, :].T

    shift = (LANES - g * R) % LANES
    dt_all = jax.nn.softplus(dt_ref[...] + dtb_ref[0:1, :])
    valid = (c * C + lax.broadcasted_iota(i32, (C, 1), 0)) < t_valid
    dt_all = jnp.where(valid, dt_all, 0.0)
    la = dt_all * (-jnp.exp(alog_ref[0:1, :]))
    cs_all = _cumsum_axis(la, 0)
    cs = pltpu.roll(cs_all, shift, axis=1)
    dt = pltpu.roll(dt_all, shift, axis=1)
    dsk = pltpu.roll(dsk_ref[...], shift, axis=1)[0:1, :]
    cs_t = cs.T
    dt_t = dt.T
    cl = cs[C - 1:C, :]
    ecs = jnp.exp(cs)
    wst = dt * jnp.exp(cl - cs)
    ecl = jnp.exp(cl)

    bm = bm_ref[...].astype(bf16)
    cm = cm_ref[...].astype(bf16)
    xs = x_ref[...]
    cb = lax.dot_general(cm, bm, _NT, preferred_element_type=f32)
    causal = lax.broadcasted_iota(i32, (C, C), 0) >= lax.broadcasted_iota(i32, (C, C), 1)
    low = lax.broadcasted_iota(i32, (C, LANES), 1) < SSD_HD

    def expand(a):
        lw = low[0:a.shape[0], :]
        return jnp.concatenate(
            [jnp.where(lw, a[:, 2 * p:2 * p + 1], a[:, 2 * p + 1:2 * p + 2]) for p in range(npair)], axis=1)

    ys = []
    for p in range(npair):
        xp = xs[:, p * LANES:(p + 1) * LANES]
        acc = None
        for hh in range(2):
            r = 2 * p + hh
            dec = jnp.exp(jnp.where(causal, cs[:, r:r + 1] - cs_t[r:r + 1, :], -jnp.inf))
            wm = (cb * dec * dt_t[r:r + 1, :]).astype(bf16)
            xm = jnp.where(low if hh == 0 else jnp.logical_not(low), xp, 0.0).astype(bf16)
            part = jnp.dot(wm, xm, preferred_element_type=f32)
            acc = part if acc is None else acc + part
        ys.append(acc)
    y = jnp.concatenate(ys, axis=1)

    st = st_sc[...]
    y = y + jnp.dot(cm, st.astype(bf16), preferred_element_type=f32) * expand(ecs)
    y = y + expand(dsk) * xs

    xw = (xs * expand(wst)).astype(bf16)
    st_new = st * expand(ecl) + lax.dot_general(bm, xw, _TN, preferred_element_type=f32)
    st_sc[...] = st_new

    yg = y * jax.nn.silu(z_ref[...])
    y_ref[...] = _rms(yg, nw_ref[...]).astype(y_ref.dtype)

    @pl.when(c == pl.num_programs(2) - 1)
    def _():
        for i in range(npair):
            so_ref[0, 0, i * LANES:(i + 1) * LANES, :] = st_new[:, i * LANES:(i + 1) * LANES].T


def ssd(src, xc, dt_bias8, a_log8, d_skip8, norm_w, s0, B, Tp, t_valid):
    G, C = SSD_GROUPS, SEQ_CHUNK
    nc = Tp // C
    GW = SSD_GW
    dt_blk = (SSD_INNER + SSD_CONV_DIM) // LANES
    xoff = SSD_INNER // LANES
    body = functools.partial(_ssd_body, C=C, t_valid=t_valid)
    par = lambda b, g, c: (0, 0)
    return pl.pallas_call(
        body,
        grid=(B, G, nc),
        in_specs=[pl.BlockSpec((C, GW), lambda b, g, c: (b * nc + c, g)),
                  pl.BlockSpec((C, GW), lambda b, g, c: (b * nc + c, g)),
                  pl.BlockSpec((C, SSD_N), lambda b, g, c: (b * nc + c, xoff + g)),
                  pl.BlockSpec((C, SSD_N), lambda b, g, c: (b * nc + c, xoff + G + g)),
                  pl.BlockSpec((C, LANES), lambda b, g, c: (b * nc + c, dt_blk)),
                  pl.BlockSpec((8, LANES), par), pl.BlockSpec((8, LANES), par), pl.BlockSpec((8, LANES), par),
                  pl.BlockSpec((1, GW), lambda b, g, c: (0, g)),
                  pl.BlockSpec((1, 1, GW, SSD_N), lambda b, g, c: (b, g, 0, 0))],
        out_specs=[pl.BlockSpec((C, GW), lambda b, g, c: (b * nc + c, g)),
                   pl.BlockSpec((1, 1, GW, SSD_N), lambda b, g, c: (b, g, 0, 0))],
        out_shape=[jax.ShapeDtypeStruct((B * Tp, SSD_INNER), bf16),
                   jax.ShapeDtypeStruct((B, G, GW, SSD_N), f32)],
        scratch_shapes=[pltpu.VMEM((SSD_N, GW), f32)],
        compiler_params=_params("parallel", "parallel", "arbitrary"),
        name="ssd_scan",
    )(src, xc, xc, xc, src, dt_bias8, a_log8, d_skip8, norm_w.reshape(1, SSD_INNER), s0)


def _topk_rows(ss, k):
    n_rows, n = ss[0].shape
    ridx = lax.broadcasted_iota(i32, ss[0].shape, 0)
    kio = lax.broadcasted_iota(i32, (k, n), 0)

    def body(it, carry):
        out = []
        for s, vals, idxs in carry:
            m = jnp.max(s, axis=0, keepdims=True)
            idx = jnp.min(jnp.where(s == m, ridx, n_rows), axis=0, keepdims=True)
            vals = jnp.where(kio == it, m, vals)
            idxs = jnp.where(kio == it, idx, idxs)
            s = jnp.where(ridx == idx, -jnp.inf, s)
            out.append((s, vals, idxs))
        return tuple(out)

    init = tuple((s, jnp.zeros((k, n), f32), jnp.zeros((k, n), i32)) for s in ss)
    res = lax.fori_loop(0, k, body, init)
    return [(vals, idxs) for _, vals, idxs in res]


_CAND_HALF = PEER_TOPK // 2


def _cand_rows(v0, v1):
    K, Hh = PEER_TOPK, _CAND_HALF
    return jnp.concatenate([v0[0:1, :] + v1] + [v0[p:p + 1, :] + v1[0:Hh, :] for p in range(1, Hh)]
                           + [v0[Hh:K, :] + v1[0:1, :]], axis=0)


def _cand_pq(pos):
    K, Hh = PEER_TOPK, _CAND_HALF
    mid = pos - K
    tail = K + Hh * (Hh - 1)
    p = jnp.where(pos < K, 0, jnp.where(pos < tail, 1 + mid // Hh, Hh + (pos - tail)))
    q = jnp.where(pos < K, pos, jnp.where(pos < tail, mid % Hh, 0))
    return p, q


def _peer_route_body(x_ref, nw_ref, wqt_ref, keys_ref, xn_ref, i1_ref, i2_ref, g_ref,
                     qt_sc, i1t_sc, i2t_sc, gt_sc, *, tm):
    K = PEER_TOPK
    xn = _rms(x_ref[...], nw_ref[...]).astype(bf16)
    xn_ref[...] = xn
    qt_sc[...] = lax.dot_general(wqt_ref[...], xn, _NT, preferred_element_type=f32)

    def head(h, carry):
        base = pl.multiple_of(h * PEER_DQ, PEER_DQ)
        rows = pl.ds(pl.multiple_of(h * K, K), K)
        for half in range(tm // LANES):
            cols = slice(half * LANES, (half + 1) * LANES)
            sts = []
            for part in range(2):
                qp = qt_sc[pl.ds(base + part * (PEER_DQ // 2), PEER_DQ // 2), cols].astype(bf16)
                sts.append(jnp.dot(keys_ref[h, part], qp, preferred_element_type=f32))
            (v0, i0), (v1, i1) = _topk_rows(sts, K)
            ((top, pos),) = _topk_rows([_cand_rows(v0, v1)], K)
            pp, qq = _cand_pq(pos)
            i1s = jnp.zeros((K, LANES), i32)
            i2s = jnp.zeros((K, LANES), i32)
            for r in range(K):
                i1s = jnp.where(pp == r, i0[r:r + 1, :], i1s)
                i2s = jnp.where(qq == r, i1[r:r + 1, :], i2s)
            e = jnp.exp(top - jnp.max(top, axis=0, keepdims=True))
            i1t_sc[rows, cols] = i1s
            i2t_sc[rows, cols] = i2s
            gt_sc[rows, cols] = e / jnp.sum(e, axis=0, keepdims=True)
        return carry

    lax.fori_loop(0, PEER_HEADS, head, 0)
    i1_ref[...] = i1t_sc[...].T
    i2_ref[...] = i2t_sc[...].T
    g_ref[...] = gt_sc[...].T


def peer_route(x, nw, wq_t, keys):
    M, D = x.shape
    tm = 256 if M % 256 == 0 else LANES
    HQ = PEER_HEADS * PEER_DQ
    S = PEER_SLOTS
    body = functools.partial(_peer_route_body, tm=tm)
    return pl.pallas_call(
        body,
        grid=(M // tm,),
        in_specs=[pl.BlockSpec((tm, D), lambda i: (i, 0)),
                  pl.BlockSpec((1, D), lambda i: (0, 0)),
                  pl.BlockSpec((HQ, D), lambda i: (0, 0)),
                  pl.BlockSpec((PEER_HEADS, 2, PEER_NKEYS, PEER_DQ // 2), lambda i: (0, 0, 0, 0))],
        out_specs=[pl.BlockSpec((tm, D), lambda i: (i, 0)),
                   pl.BlockSpec((tm, S), lambda i: (i, 0)),
                   pl.BlockSpec((tm, S), lambda i: (i, 0)),
                   pl.BlockSpec((tm, S), lambda i: (i, 0))],
        out_shape=[jax.ShapeDtypeStruct((M, D), bf16), jax.ShapeDtypeStruct((M, S), i32),
                   jax.ShapeDtypeStruct((M, S), i32), jax.ShapeDtypeStruct((M, S), f32)],
        scratch_shapes=[pltpu.VMEM((HQ, tm), f32), pltpu.VMEM((S, tm), i32), pltpu.VMEM((S, tm), i32),
                        pltpu.VMEM((S, tm), f32)],
        compiler_params=_params("parallel"),
        name="peer_route",
    )(x, nw.reshape(1, D), wq_t, keys)


def _peer_hidden_body(xn_ref, u_ref, i1_ref, i2_ref, g_ref, c_ref, h_sc, s_sc, *, te):
    e = pl.program_id(1)

    @pl.when(e == 0)
    def _():
        h_sc[...] = jnp.zeros(h_sc.shape, f32)
        s_sc[1] = jnp.zeros(s_sc.shape[1:], f32)

    slot = e % 2
    NK = PEER_NKEYS
    nb = te // NK
    i1 = i1_ref[...]
    i2 = i2_ref[...]
    h = h_sc[...]
    for sub in range(nb):
        got = jnp.take_along_axis(s_sc[1 - slot, :, sub * NK:(sub + 1) * NK], i2, axis=1)
        h = jnp.where(i1 == (e - 1) * nb + sub, got, h)
    h_sc[...] = h
    s_sc[slot] = lax.dot_general(xn_ref[...], u_ref[...], _NT, preferred_element_type=f32)

    @pl.when(e == pl.num_programs(1) - 1)
    def _():
        act = 0.5 * h * (1.0 + lax.erf(h * (2.0 ** -0.5)))
        c_ref[...] = g_ref[...] * act


def peer_hidden(xn, u, i1, i2, g):
    M, D = xn.shape
    E = u.shape[0]
    S = PEER_SLOTS
    tm, te = _tile(M, 512), 512
    ne = E // te
    tok = lambda i, e: (i, 0)
    body = functools.partial(_peer_hidden_body, te=te)
    return pl.pallas_call(
        body,
        grid=(M // tm, ne + 1),
        in_specs=[pl.BlockSpec((tm, D), tok),
                  pl.BlockSpec((te, D), lambda i, e: (jnp.minimum(e, ne - 1), 0)),
                  pl.BlockSpec((tm, S), tok), pl.BlockSpec((tm, S), tok), pl.BlockSpec((tm, S), tok)],
        out_specs=pl.BlockSpec((tm, S), tok),
        out_shape=jax.ShapeDtypeStruct((M, S), f32),
        scratch_shapes=[pltpu.VMEM((tm, S), f32), pltpu.VMEM((2, tm, te), f32)],
        compiler_params=_params("parallel", "arbitrary"),
        name="peer_hidden",
    )(xn, u, i1, i2, g)


G_PITCH = PEER_NKEYS + 4
G_UNROLL = 8


def _peer_out_body(i1_ref, i2_ref, c_ref, v_ref, x_ref, o_ref, g_sc, *, tm, te):
    e = pl.program_id(1)
    NK = PEER_NKEYS

    @pl.when(e == 0)
    def _():
        o_ref[...] = x_ref[...]
        sub = lax.broadcasted_iota(i32, (NK, PEER_SLOTS), 0)

        def tok_group(gi, carry):
            for u in range(G_UNROLL):
                n = gi * G_UNROLL + u
                r1 = i1_ref[pl.ds(n, 1), :]
                r2 = i2_ref[pl.ds(n, 1), :]
                rc = c_ref[pl.ds(n, 1), :]
                at = jnp.where(sub == r1, rc, 0.0).astype(bf16)
                bt = jnp.where(sub == r2, 1.0, 0.0).astype(bf16)
                g_sc[pl.ds(n * G_PITCH, NK), :] = lax.dot_general(at, bt, _NT, preferred_element_type=f32)
            return carry

        lax.fori_loop(0, tm // G_UNROLL, tok_group, 0)

    nb = te // NK
    gcat = jnp.concatenate([g_sc[pl.ds(e * nb + s, tm, stride=G_PITCH), :] for s in range(nb)],
                           axis=1).astype(bf16)
    o_ref[...] += jnp.dot(gcat, v_ref[...], preferred_element_type=f32)


def peer_out(i1, i2, c, v, x):
    M, D = x.shape
    E = v.shape[0]
    S = PEER_SLOTS
    tm, te = _tile(M, 256), 1024
    tok = lambda i, e: (i, 0)
    body = functools.partial(_peer_out_body, tm=tm, te=te)
    return pl.pallas_call(
        body,
        grid=(M // tm, E // te),
        in_specs=[pl.BlockSpec((tm, S), tok), pl.BlockSpec((tm, S), tok), pl.BlockSpec((tm, S), tok),
                  pl.BlockSpec((te, D), lambda i, e: (e, 0)),
                  pl.BlockSpec((tm, D), tok)],
        out_specs=pl.BlockSpec((tm, D), tok),
        out_shape=jax.ShapeDtypeStruct((M, D), f32),
        scratch_shapes=[pltpu.VMEM((tm * G_PITCH, PEER_NKEYS), f32)],
        compiler_params=_params("parallel", "arbitrary"),
        name="peer_out",
    )(i1, i2, c, v, x)


def peer_ffn(x, nw, wq_t, keys, u, v):
    M, D = x.shape
    Mp = -(-M // LANES) * LANES
    xp = jnp.pad(x, ((0, Mp - M), (0, 0))) if Mp != M else x
    xn, i1, i2, g = peer_route(xp, nw, wq_t, keys)
    c = peer_hidden(xn, u, i1, i2, g)
    out = peer_out(i1, i2, c, v, xp)
    return out[:M] if Mp != M else out


def _pad_time(a, B, T, Tp):
    if Tp == T:
        return a
    W = a.shape[1]
    return jnp.pad(a.reshape(B, T, W), ((0, 0), (0, Tp - T), (0, 0))).reshape(B * Tp, W)


def _unpad_time(a, B, T, Tp):
    if Tp == T:
        return a
    W = a.shape[1]
    return a.reshape(B, Tp, W)[:, :T].reshape(B * T, W)


def _trunk(x, past, hgrn0, ssm0, conv0, wts):
    B, T, D = x.shape
    M = B * T
    H = FOX_HEADS
    if T % SEQ_CHUNK == 0:
        Tp = T
    else:
        assert T <= SEQ_CHUNK
        Tp = SEQ_CHUNK
    x2 = x.reshape(M, D)

    proj = norm_matmul(x2, wts["norm_mix"][0], wts["w_in_a"], "in_proj_a")
    ff = proj[:, 7 * FOX_W:7 * FOX_W + H]
    fk = proj[:, FOX_W:2 * FOX_W].reshape(B, T, H, FOX_HD)
    fv = proj[:, 2 * FOX_W:3 * FOX_W].reshape(B, T, H, FOX_HD)
    if past is None:
        ff_t = ff.reshape(B, T, H).transpose(0, 2, 1).reshape(B * H, T)
        lf_t, c_t = fox_gate(ff_t, jnp.tile(wts["b_fox_f"], B).reshape(B * H, 1))
        flogf = lf_t.reshape(B, H, T).transpose(0, 2, 1)
        c_bth = c_t.reshape(B, H, T).transpose(0, 2, 1)
        fo = fox_prompt_attn(proj, c_bth, c_t.reshape(B * H, 1, T), B, T)
    else:
        assert T * H <= LANES
        k_pool, v_pool, lf_pool, page_table = past
        n_phys = k_pool.shape[0]
        pad = LANES - T * H
        ff_l = jnp.pad(ff.reshape(B, T * H), ((0, 0), (0, pad)))
        lf_l, c_l = fox_gate(ff_l, jnp.tile(wts["b_fox_f"], LANES // H).reshape(1, LANES), stride=H)
        flogf = lf_l[:, :T * H].reshape(B, T, H)
        sfx_pool, tot_pool = page_suffix_sums(lf_pool.reshape(n_phys, PAGE * H), H)
        q_r = proj[:, 0:FOX_W].reshape(B, T * H, FOX_HD).astype(bf16)
        k_new = jnp.pad(proj[:, FOX_W:2 * FOX_W].reshape(B, T * H, FOX_HD), ((0, 0), (0, pad), (0, 0)))
        v_new = jnp.pad(proj[:, 2 * FOX_W:3 * FOX_W].reshape(B, T * H, FOX_HD), ((0, 0), (0, pad), (0, 0)))
        fo = fox_decode_attn(q_r, k_new, v_new, c_l.reshape(B, 1, LANES),
                             k_pool.reshape(n_phys, PAGE * H, FOX_HD), v_pool.reshape(n_phys, PAGE * H, FOX_HD),
                             sfx_pool.reshape(n_phys, 1, PAGE * H), tot_pool.reshape(n_phys, 1, PAGE * H),
                             page_table).reshape(M, FOX_W)

    if Tp == T:
        hsrc, hcb0 = proj, 3 * H
    else:
        hsrc, hcb0 = _pad_time(proj[:, 3 * FOX_W:3 * FOX_W + 4 * HGRN_W], B, T, Tp), 0
    ho, h_state = gla(hsrc, hcb0, wts["hgrn_lb_logits"], wts["hgrn_norm_w"], hgrn0, B, Tp, T, 0)
    ho = _unpad_time(ho, B, T, Tp)
    x2 = matmul_residual(jnp.concatenate([fo, ho], axis=1), wts["w_out_a"], x2, "out_proj_a")
    x2 = peer_ffn(x2, wts["norm_ffn"][0], wts["peer_wq_t"][0], wts["peer_keys"][0], wts["peer_u"][0],
                  wts["peer_v"][0])

    projc = norm_matmul(x2, wts["norm_mix"][1], wts["w_in_c"], "in_proj_c")
    xbc_raw = projc[:, SSD_INNER:SSD_INNER + SSD_CONV_DIM].reshape(B, T, SSD_CONV_DIM)
    if T >= SSD_CONV - 1:
        conv_state = xbc_raw[:, T - (SSD_CONV - 1):]
    else:
        conv_state = jnp.concatenate([conv0, xbc_raw], axis=1)[:, -(SSD_CONV - 1):]
    csrc = _pad_time(projc, B, T, Tp)
    conv0_p = jnp.pad(conv0, ((0, 0), (8 - (SSD_CONV - 1), 0), (0, 0)))
    xc = conv_silu(csrc, SSD_INNER // 512, wts["conv_w"], wts["conv_b"], conv0_p, B, Tp)
    yg, s_state = ssd(csrc, xc, wts["dt_bias8"], wts["a_log8"], wts["d_skip8"], wts["ssd_norm_w"],
                      ssm0.reshape(B, SSD_GROUPS, SSD_GW, SSD_N), B, Tp, T)
    x2 = matmul_residual(_unpad_time(yg, B, T, Tp), wts["w_out_c"], x2, "out_proj_c")
    x2 = peer_ffn(x2, wts["norm_ffn"][1], wts["peer_wq_t"][1], wts["peer_keys"][1], wts["peer_u"][1],
                  wts["peer_v"][1])

    y = final_norm(x2, wts["norm_final"]).reshape(B, T, D)
    return (y, fk[None], fv[None], flogf[None], h_state[None],
            s_state.reshape(B, SSD_HEADS, SSD_HD, SSD_N)[None], conv_state[None])


def _row8(a):
    return jnp.tile(jnp.pad(a.astype(f32), (0, LANES - a.shape[0]))[None, :], (8, 1))


def kernel(x_prompt, x_sample, cache_fox_k, cache_fox_v, cache_fox_logf, state_hgrn, state_ssm, state_conv, page_table, norm_mix, norm_ffn, norm_final, w_in_a, b_fox_f, hgrn_lb_logits, hgrn_norm_w, w_out_a, w_in_c, conv_w, conv_b, dt_bias, a_log, d_skip, ssd_norm_w, w_out_c, peer_wq, peer_keys, peer_u, peer_v):
    D = x_prompt.shape[-1]
    wa = w_in_a[0]
    n_main = 3 * FOX_W
    wa_r = jnp.concatenate(
        [wa[:, :n_main], wa[:, n_main + FOX_HEADS:],
         jnp.pad(wa[:, n_main:n_main + FOX_HEADS], ((0, 0), (0, N_TAIL - FOX_HEADS)))], axis=1).astype(bf16)
    wc = w_in_c[0]
    wc_r = jnp.pad(wc, ((0, 0), (0, N_TAIL - SSD_HEADS))).astype(bf16)
    wts = dict(
        norm_mix=norm_mix, norm_ffn=norm_ffn, norm_final=norm_final,
        w_in_a=wa_r, b_fox_f=b_fox_f[0], hgrn_lb_logits=hgrn_lb_logits, hgrn_norm_w=hgrn_norm_w[0],
        w_out_a=w_out_a[0].astype(bf16),
        w_in_c=wc_r, conv_w=conv_w[0], conv_b=conv_b[0],
        dt_bias8=_row8(dt_bias[0]), a_log8=_row8(a_log[0]), d_skip8=_row8(d_skip[0]),
        ssd_norm_w=ssd_norm_w[0], w_out_c=w_out_c[0].astype(bf16),
        peer_wq_t=jnp.swapaxes(peer_wq, 1, 2).astype(bf16), peer_keys=peer_keys.astype(bf16),
        peer_u=peer_u.astype(bf16), peer_v=peer_v.astype(bf16),
    )
    Bp = x_prompt.shape[0]
    hgrn0_p = jnp.zeros((Bp, HGRN_HEADS, HGRN_DK, HGRN_DV), f32)
    ssm0_p = jnp.zeros((Bp, SSD_HEADS, SSD_HD, SSD_N), f32)
    conv0_p = jnp.zeros((Bp, SSD_CONV - 1, SSD_CONV_DIM), f32)
    y_p, fk_p, fv_p, fl_p, h_p, s_p, c_p = _trunk(x_prompt, None, hgrn0_p, ssm0_p, conv0_p, wts)
    past = (cache_fox_k[0], cache_fox_v[0], cache_fox_logf[0], page_table)
    y_s, fk_s, fv_s, fl_s, h_s, s_s, c_s = _trunk(x_sample, past, state_hgrn[0], state_ssm[0], state_conv[0], wts)
    return (y_p, y_s, fk_p, fv_p, fl_p, fk_s, fv_s, fl_s, h_p, h_s, s_p, s_s, c_p, c_s)
```

```python
import functools
import math

import jax
import jax.numpy as jnp
from jax import lax
from jax.experimental import pallas as pl
from jax.experimental.pallas import tpu as pltpu

f32 = jnp.float32
bf16 = jnp.bfloat16
i32 = jnp.int32

EPS = 1e-6
LANES = 128
VMEM_LIMIT = 48 * 1024 * 1024

FOX_HD = 128
FOX_HEADS = 8
FOX_W = FOX_HEADS * FOX_HD
HGRN_HEADS = 8
HGRN_DK = 128
HGRN_DV = 128
HGRN_W = HGRN_HEADS * HGRN_DK
SSD_HD = 64
SSD_HEADS = 64
SSD_GROUPS = 8
SSD_R = SSD_HEADS // SSD_GROUPS
SSD_N = 128
SSD_INNER = SSD_HEADS * SSD_HD
SSD_GW = SSD_INNER // SSD_GROUPS
SSD_CONV = 4
SSD_CONV_DIM = SSD_INNER + 2 * SSD_GROUPS * SSD_N
PEER_HEADS = 8
PEER_NKEYS = 128
PEER_DQ = 256
PEER_TOPK = 16
PEER_SLOTS = PEER_HEADS * PEER_TOPK
PAGE = 128
SEQ_CHUNK = 128
DIAG = 16
GLA_SUB = 64
GLA_SAFE_LOG = 80.0
N_TAIL = 512

_NT = (((1,), (1,)), ((), ()))
_TN = (((0,), (0,)), ((), ()))


def _params(*sem):
    return pltpu.CompilerParams(dimension_semantics=sem, vmem_limit_bytes=VMEM_LIMIT)


def _tile(n, target):
    t = min(n, target)
    while n % t:
        t -= 8
    return t


def _cumsum_axis(x, axis, stride=1):
    n = x.shape[axis]
    idx = lax.broadcasted_iota(i32, x.shape, axis)
    s = stride
    while s < n:
        x = x + jnp.where(idx >= s, pltpu.roll(x, s, axis=axis), 0.0)
        s *= 2
    return x


def _row_reduce(x, combine, reduce):
    acc = x[:, 0:LANES]
    for i in range(1, x.shape[1] // LANES):
        acc = combine(acc, x[:, i * LANES:(i + 1) * LANES])
    return reduce(acc, axis=1, keepdims=True)


def _rms(x, w):
    ms = jnp.mean(x * x, axis=-1, keepdims=True)
    return x * lax.rsqrt(ms + EPS) * w


def _norm_matmul_body(x_ref, nw_ref, w_ref, o_ref, xn_ref):
    @pl.when(pl.program_id(1) == 0)
    def _():
        xn_ref[...] = _rms(x_ref[...], nw_ref[...]).astype(bf16)

    o_ref[...] = jnp.dot(xn_ref[...], w_ref[...], preferred_element_type=f32)


def norm_matmul(x, nw, w, name):
    M, K = x.shape
    N = w.shape[1]
    tm, tn = _tile(M, 1024), _tile(N, 768)
    return pl.pallas_call(
        _norm_matmul_body,
        grid=(M // tm, N // tn),
        in_specs=[pl.BlockSpec((tm, K), lambda i, j: (i, 0)),
                  pl.BlockSpec((1, K), lambda i, j: (0, 0)),
                  pl.BlockSpec((K, tn), lambda i, j: (0, j))],
        out_specs=pl.BlockSpec((tm, tn), lambda i, j: (i, j)),
        out_shape=jax.ShapeDtypeStruct((M, N), f32),
        scratch_shapes=[pltpu.VMEM((tm, K), bf16)],
        compiler_params=_params("parallel", "arbitrary"),
        name=name,
    )(x, nw.reshape(1, K), w)


def _matmul_res_body(a_ref, w_ref, r_ref, o_ref):
    o_ref[...] = r_ref[...] + jnp.dot(a_ref[...].astype(bf16), w_ref[...], preferred_element_type=f32)


def matmul_residual(a, w, res, name):
    M, K = a.shape
    N = w.shape[1]
    tm, tn = _tile(M, 1024), _tile(N, 512)
    return pl.pallas_call(
        _matmul_res_body,
        grid=(M // tm, N // tn),
        in_specs=[pl.BlockSpec((tm, K), lambda i, j: (i, 0)),
                  pl.BlockSpec((K, tn), lambda i, j: (0, j)),
                  pl.BlockSpec((tm, tn), lambda i, j: (i, j))],
        out_specs=pl.BlockSpec((tm, tn), lambda i, j: (i, j)),
        out_shape=jax.ShapeDtypeStruct((M, N), f32),
        compiler_params=_params("parallel", "arbitrary"),
        name=name,
    )(a, w, res)


def _final_norm_body(x_ref, nw_ref, o_ref):
    o_ref[...] = _rms(x_ref[...], nw_ref[...])


def final_norm(x, nw):
    M, K = x.shape
    tm = _tile(M, 512)
    return pl.pallas_call(
        _final_norm_body,
        grid=(M // tm,),
        in_specs=[pl.BlockSpec((tm, K), lambda i: (i, 0)), pl.BlockSpec((1, K), lambda i: (0, 0))],
        out_specs=pl.BlockSpec((tm, K), lambda i: (i, 0)),
        out_shape=jax.ShapeDtypeStruct((M, K), f32),
        compiler_params=_params("parallel"),
        name="final_norm",
    )(x, nw.reshape(1, K))


def _fox_gate_body(ff_ref, b_ref, lf_ref, c_ref, *, stride):
    lf = jax.nn.log_sigmoid(ff_ref[...] + b_ref[...])
    lf_ref[...] = lf
    c_ref[...] = _cumsum_axis(lf, 1, stride)


def fox_gate(ff, bias, stride=1):
    R, T = ff.shape
    shp = jax.ShapeDtypeStruct((R, T), f32)
    return pl.pallas_call(functools.partial(_fox_gate_body, stride=stride), out_shape=[shp, shp], name="fox_gate",
                          compiler_params=pltpu.CompilerParams(vmem_limit_bytes=VMEM_LIMIT))(ff, bias)


def _fox_attn_body(q_ref, k_ref, v_ref, ct_ref, cs_ref, o_ref, vt_sc, cs_sc, m_sc, l_sc, acc_sc, *, tq, scale, hpb):
    hg = pl.program_id(1)
    qi = pl.program_id(2)
    D = FOX_HD
    T = k_ref.shape[0]

    @pl.when(qi == 0)
    def _():
        cs = cs_ref[0]
        lane = lax.broadcasted_iota(i32, cs.shape, 1)
        for j in range(hpb):
            col = jnp.sum(jnp.where(lane == hg * hpb + j, cs, 0.0), axis=1, keepdims=True)
            cs_sc[j] = jnp.broadcast_to(col, (T, LANES))
            for t in range(T // LANES):
                blk = v_ref[t * LANES:(t + 1) * LANES, j * D:(j + 1) * D]
                vt_sc[j, :, t * LANES:(t + 1) * LANES] = blk.T.astype(bf16)

    q0 = pl.multiple_of(qi * tq, tq)
    qs, cts = [], []
    for j in range(hpb):
        qs.append(q_ref[:, j * D:(j + 1) * D].astype(bf16))
        cts.append(ct_ref[j, :, pl.ds(q0, tq)])
        m_sc[j] = jnp.full((1, tq), -jnp.inf, f32)
        l_sc[j] = jnp.zeros((1, tq), f32)
        acc_sc[j] = jnp.zeros((D, tq), f32)
    causal = lax.broadcasted_iota(i32, (tq, tq), 0) <= lax.broadcasted_iota(i32, (tq, tq), 1)

    def block(ki, masked):
        start = pl.multiple_of(ki * tq, tq)
        for j in range(hpb):
            k = k_ref[pl.ds(start, tq), j * D:(j + 1) * D].astype(bf16)
            st = lax.dot_general(k, qs[j], _NT, preferred_element_type=f32) * scale
            csb = cs_sc[j, pl.ds(start, tq), :]
            st = st + cts[j] - jnp.concatenate([csb] * (tq // LANES), axis=1)
            if masked:
                st = jnp.where(causal, st, -jnp.inf)
            m_prev = m_sc[j]
            m_new = jnp.maximum(m_prev, jnp.max(st, axis=0, keepdims=True))
            alpha = jnp.exp(m_prev - m_new)
            p = jnp.exp(st - m_new)
            l_sc[j] = alpha * l_sc[j] + jnp.sum(p, axis=0, keepdims=True)
            acc_sc[j] = alpha * acc_sc[j] + jnp.dot(vt_sc[j, :, pl.ds(start, tq)], p.astype(bf16),
                                                    preferred_element_type=f32)
            m_sc[j] = m_new

    def body(ki, carry):
        block(ki, False)
        return carry

    lax.fori_loop(0, qi, body, 0)
    block(qi, True)
    for j in range(hpb):
        o_t = acc_sc[j] / l_sc[j]
        for t in range(tq // LANES):
            o_ref[t * LANES:(t + 1) * LANES, j * D:(j + 1) * D] = o_t[:, t * LANES:(t + 1) * LANES].T


def fox_prompt_attn(proj, c_bth, c_rows, B, T):
    tq = _tile(T, 256)
    nq = T // tq
    H = FOX_HEADS
    hpb = 2
    W = hpb * FOX_HD
    G = H // hpb
    body = functools.partial(_fox_attn_body, tq=tq, scale=FOX_HD ** -0.5, hpb=hpb)
    return pl.pallas_call(
        body,
        grid=(B, G, nq),
        in_specs=[pl.BlockSpec((tq, W), lambda b, g, qi: (b * nq + qi, g)),
                  pl.BlockSpec((T, W), lambda b, g, qi: (b, G + g)),
                  pl.BlockSpec((T, W), lambda b, g, qi: (b, 2 * G + g)),
                  pl.BlockSpec((hpb, 1, T), lambda b, g, qi: (b * G + g, 0, 0)),
                  pl.BlockSpec((1, T, H), lambda b, g, qi: (b, 0, 0))],
        out_specs=pl.BlockSpec((tq, W), lambda b, g, qi: (b * nq + qi, g)),
        out_shape=jax.ShapeDtypeStruct((B * T, FOX_W), f32),
        scratch_shapes=[pltpu.VMEM((hpb, FOX_HD, T), bf16), pltpu.VMEM((hpb, T, LANES), f32),
                        pltpu.VMEM((hpb, 1, tq), f32), pltpu.VMEM((hpb, 1, tq), f32),
                        pltpu.VMEM((hpb, FOX_HD, tq), f32)],
        compiler_params=_params("parallel", "parallel", "arbitrary"),
        name="fox_prompt_attn",
    )(proj, proj, proj, c_rows, c_bth)


def _suffix_body(x_ref, sfx_ref, tot_ref, *, stride):
    x = x_ref[...]
    lane = lax.broadcasted_iota(i32, x.shape, 1)
    n = x.shape[1]
    y = jnp.where(lane < n - stride, pltpu.roll(x, n - stride, axis=1), 0.0)
    t = x
    s = stride
    while s < n:
        y = y + jnp.where(lane < n - s, pltpu.roll(y, n - s, axis=1), 0.0)
        t = t + pltpu.roll(t, s, axis=1)
        s *= 2
    sfx_ref[...] = y
    tot_ref[...] = t


def page_suffix_sums(lf_rows, stride):
    R, n = lf_rows.shape
    tr = _tile(R, 512)
    shp = jax.ShapeDtypeStruct((R, n), f32)
    return pl.pallas_call(
        functools.partial(_suffix_body, stride=stride),
        grid=(R // tr,),
        in_specs=[pl.BlockSpec((tr, n), lambda i: (i, 0))],
        out_specs=[pl.BlockSpec((tr, n), lambda i: (i, 0)), pl.BlockSpec((tr, n), lambda i: (i, 0))],
        out_shape=[shp, shp],
        compiler_params=_params("parallel"),
        name="fox_page_suffix",
    )(lf_rows)


def _fox_dec_body(pt_ref, q_ref, kn_ref, vn_ref, cn_ref, *refs, scale, G):
    del pt_ref
    kp_refs, vp_refs = refs[0:G], refs[G:2 * G]
    sfx_refs, tot_refs = refs[2 * G:3 * G], refs[3 * G:4 * G]
    o_ref, m_sc, l_sc, acc_sc, carry_sc, cc_sc = refs[4 * G:]
    H = FOX_HEADS
    j = pl.program_id(1)
    q = q_ref[0]

    def partial_softmax(k, v, bias):
        s = lax.dot_general(q, k.astype(bf16), _NT, preferred_element_type=f32) * scale + bias
        m = _row_reduce(s, jnp.maximum, jnp.max)
        p = jnp.exp(s - m)
        return m, _row_reduce(p, jnp.add, jnp.sum), jnp.dot(p.astype(bf16), v.astype(bf16),
                                                            preferred_element_type=f32)

    def merge(parts):
        m_prev = m_sc[...]
        m_new = m_prev
        for m, _, _ in parts:
            m_new = jnp.maximum(m_new, m)
        alpha = jnp.exp(m_prev - m_new)
        l = alpha * l_sc[...]
        acc = alpha * acc_sc[...]
        for m, lp, ap in parts:
            w = jnp.exp(m - m_new)
            l = l + w * lp
            acc = acc + w * ap
        m_sc[...] = m_new
        l_sc[...] = l
        acc_sc[...] = acc

    def step(k, v, bias):
        merge([partial_softmax(k, v, bias)])

    @pl.when(j == 0)
    def _():
        m_sc[...] = jnp.full(m_sc.shape, -jnp.inf, f32)
        l_sc[...] = jnp.zeros(l_sc.shape, f32)
        acc_sc[...] = jnp.zeros(acc_sc.shape, f32)
        carry_sc[...] = jnp.zeros(carry_sc.shape, f32)
        cn = cn_ref[0]
        shape = (q.shape[0], cn.shape[1])
        row = lax.broadcasted_iota(i32, shape, 0)
        col = lax.broadcasted_iota(i32, shape, 1)
        cc = jnp.sum(jnp.where(row == col, cn, 0.0), axis=1, keepdims=True)
        cc_sc[...] = cc
        ok = jnp.logical_and(col % H == row % H, col // H <= row // H)
        step(kn_ref[0], vn_ref[0], jnp.where(ok, cc - cn, -jnp.inf))

    @pl.when(j > 0)
    def _():
        carry = carry_sc[...]
        shape = (q.shape[0], carry.shape[1])
        same_head = lax.broadcasted_iota(i32, shape, 1) % H == lax.broadcasted_iota(i32, shape, 0) % H
        cc = cc_sc[...]
        parts = []
        for g in range(G):
            bias = jnp.where(same_head, cc + (carry + sfx_refs[g][0]), -jnp.inf)
            parts.append(partial_softmax(kp_refs[g][0], vp_refs[g][0], bias))
            carry = carry + tot_refs[g][0]
        merge(parts)
        carry_sc[...] = carry

    @pl.when(j == pl.num_programs(1) - 1)
    def _():
        o_ref[0] = acc_sc[...] / l_sc[...]


def fox_decode_attn(q, k_new, v_new, c_new, k_pool, v_pool, sfx_pool, tot_pool, page_table):
    Bd, n_pages = page_table.shape
    R, D = q.shape[1], q.shape[2]
    PW = k_pool.shape[1]
    G = math.gcd(n_pages, 4)

    def page(g):
        return lambda b, j, pt: (pt[b, n_pages - 1 - (jnp.maximum(j, 1) - 1) * G - g], 0, 0)

    same = lambda b, j, pt: (b, 0, 0)
    grid_spec = pltpu.PrefetchScalarGridSpec(
        num_scalar_prefetch=1,
        grid=(Bd, n_pages // G + 1),
        in_specs=([pl.BlockSpec((1, R, D), same),
                   pl.BlockSpec((1, LANES, D), same),
                   pl.BlockSpec((1, LANES, D), same),
                   pl.BlockSpec((1, 1, LANES), same)]
                  + [pl.BlockSpec((1, PW, D), page(g)) for g in range(G)]
                  + [pl.BlockSpec((1, PW, D), page(g)) for g in range(G)]
                  + [pl.BlockSpec((1, 1, PW), page(g)) for g in range(G)]
                  + [pl.BlockSpec((1, 1, PW), page(g)) for g in range(G)]),
        out_specs=pl.BlockSpec((1, R, D), same),
        scratch_shapes=[pltpu.VMEM((R, 1), f32), pltpu.VMEM((R, 1), f32), pltpu.VMEM((R, D), f32),
                        pltpu.VMEM((1, PW), f32), pltpu.VMEM((R, 1), f32)],
    )
    body = functools.partial(_fox_dec_body, scale=FOX_HD ** -0.5, G=G)
    return pl.pallas_call(
        body,
        grid_spec=grid_spec,
        out_shape=jax.ShapeDtypeStruct((Bd, R, D), f32),
        compiler_params=_params("parallel", "arbitrary"),
        name="fox_decode_attn",
    )(page_table, q, k_new, v_new, c_new, *([k_pool] * G), *([v_pool] * G), *([sfx_pool] * G),
      *([tot_pool] * G))


def _gla_body(q_ref, f_ref, i_ref, g_ref, lbl_ref, nw_ref, s0_ref, o_ref, so_ref,
              s_sc, q_sc, k_sc, b_sc, v_sc, oa_sc, *, C, t_valid, layer_j, HB):
    c = pl.program_id(2)
    SUB = GLA_SUB
    K = HGRN_DK

    @pl.when(c == 0)
    def _():
        for j in range(HB):
            s_sc[j] = s0_ref[0, j].T

    lg = lbl_ref[...]
    e = jnp.exp(lg - jnp.max(lg, axis=0, keepdims=True))
    sm = e / jnp.sum(e, axis=0, keepdims=True)
    lb = jnp.sum(sm[0:layer_j + 1], axis=0, keepdims=True)

    z = f_ref[...]
    hlogf = jnp.log(lb + (1.0 - lb) * jax.nn.sigmoid(z))
    hk = (1.0 - lb) * jax.nn.sigmoid(-z)
    valid = (c * C + lax.broadcasted_iota(i32, (C, 1), 0)) < t_valid
    lc = jnp.where(valid, hlogf, 0.0)
    kk = jnp.where(valid, hk, 0.0)
    subs = list(range(0, C, SUB))
    bs = [_cumsum_axis(lc[lo:lo + SUB, :], 0) for lo in subs]
    bl_min = bs[0][SUB - 1:SUB, :]
    for b_ in bs[1:]:
        bl_min = jnp.minimum(bl_min, b_[SUB - 1:SUB, :])
    in_range = jnp.min(bl_min) >= -GLA_SAFE_LOG
    tril = lax.broadcasted_iota(i32, (SUB, SUB), 1) <= lax.broadcasted_iota(i32, (SUB, SUB), 0)

    @pl.when(in_range)
    def _():
        for j in range(HB):
            cols = slice(j * K, (j + 1) * K)
            st = s_sc[j]
            for si, lo in enumerate(subs):
                rows = slice(lo, lo + SUB)
                b = bs[si][:, cols]
                bl = b[SUB - 1:SUB, :]
                ks = kk[rows, cols]
                vs = i_ref[rows, cols].astype(bf16)
                qd = (q_ref[rows, cols] * jnp.exp(b)).astype(bf16)
                ki = (ks * jnp.exp(-b)).astype(bf16)
                a = jnp.where(tril, lax.dot_general(qd, ki, _NT, preferred_element_type=f32), 0.0)
                o = (lax.dot_general(qd, st.astype(bf16), _NT, preferred_element_type=f32)
                     + jnp.dot(a.astype(bf16), vs, preferred_element_type=f32))
                kd = (ks * jnp.exp(bl - b)).astype(bf16)
                st = st * jnp.exp(bl) + lax.dot_general(vs, kd, _TN, preferred_element_type=f32)
                o_ref[rows, cols] = _rms(o, nw_ref[...]) * jax.nn.silu(g_ref[rows, cols])
            s_sc[j] = st

    def off_diag(lo, hi):
        if hi - lo <= DIAG:
            return
        mid = (lo + hi) // 2
        r = b_sc[pl.ds(mid - 1, 1), :]
        qh = q_sc[mid:hi, :] * jnp.exp(b_sc[mid:hi, :] - r)
        kh = k_sc[lo:mid, :] * jnp.exp(r - b_sc[lo:mid, :])
        a = lax.dot_general(qh.astype(bf16), kh.astype(bf16), _NT, preferred_element_type=f32)
        oa_sc[mid:hi, :] += jnp.dot(a.astype(bf16), v_sc[lo:mid, :].astype(bf16), preferred_element_type=f32)
        off_diag(lo, mid)
        off_diag(mid, hi)

    srow = lax.broadcasted_iota(i32, (DIAG, 1), 0)

    def diag(i, carry):
        lo = pl.multiple_of(i * DIAG, DIAG)
        qb = q_sc[pl.ds(lo, DIAG), :]
        kb = k_sc[pl.ds(lo, DIAG), :]
        bb = b_sc[pl.ds(lo, DIAG), :]
        vb = v_sc[pl.ds(lo, DIAG), :]
        ob = jnp.zeros((DIAG, vb.shape[1]), f32)
        for t in range(DIAG):
            dec = jnp.exp(jnp.where(srow <= t, bb[t:t + 1, :] - bb, -jnp.inf))
            w = jnp.sum(qb[t:t + 1, :] * kb * dec, axis=1, keepdims=True)
            ot = jnp.sum(w * vb, axis=0, keepdims=True)
            ob = jnp.where(srow == t, ot, ob)
        oa_sc[pl.ds(lo, DIAG), :] += ob
        return carry

    @pl.when(jnp.logical_not(in_range))
    def _():
        for j in range(HB):
            cols = slice(j * K, (j + 1) * K)
            q_sc[...] = q_ref[:, cols]
            k_sc[...] = kk[:, cols]
            v_sc[...] = i_ref[:, cols]
            for si, lo in enumerate(subs):
                rows = slice(lo, lo + SUB)
                b = bs[si][:, cols]
                b_sc[rows, :] = b
                bl = b[SUB - 1:SUB, :]
                st = s_sc[j]
                qd = (q_sc[rows, :] * jnp.exp(b)).astype(bf16)
                oa_sc[rows, :] = lax.dot_general(qd, st.astype(bf16), _NT, preferred_element_type=f32)
                off_diag(lo, lo + SUB)
                lax.fori_loop(lo // DIAG, (lo + SUB) // DIAG, diag, 0)
                kd = (k_sc[rows, :] * jnp.exp(bl - b)).astype(bf16)
                s_sc[j] = st * jnp.exp(bl) + lax.dot_general(v_sc[rows, :].astype(bf16), kd, _TN,
                                                             preferred_element_type=f32)
            o_ref[:, cols] = _rms(oa_sc[...], nw_ref[...]) * jax.nn.silu(g_ref[:, cols])

    @pl.when(c == pl.num_programs(2) - 1)
    def _():
        for j in range(HB):
            so_ref[0, j] = s_sc[j].T


def gla(src, cb0, lb_logits, norm_w, s0, B, Tp, t_valid, layer_j):
    H, C = HGRN_HEADS, SEQ_CHUNK
    HB = 4
    G = H // HB
    W = HB * HGRN_DK
    nc = Tp // C
    L = lb_logits.shape[0]
    assert cb0 % HB == 0

    def col(g):
        return pl.BlockSpec((C, W), lambda b, h, c: (b * nc + c, (cb0 + g * H) // HB + h))

    body = functools.partial(_gla_body, C=C, t_valid=t_valid, layer_j=layer_j, HB=HB)
    return pl.pallas_call(
        body,
        grid=(B, G, nc),
        in_specs=[col(0), col(1), col(2), col(3),
                  pl.BlockSpec((L, W), lambda b, h, c: (0, h)),
                  pl.BlockSpec((1, HGRN_DV), lambda b, h, c: (0, 0)),
                  pl.BlockSpec((1, HB, HGRN_DK, HGRN_DV), lambda b, h, c: (b, h, 0, 0))],
        out_specs=[pl.BlockSpec((C, W), lambda b, h, c: (b * nc + c, h)),
                   pl.BlockSpec((1, HB, HGRN_DK, HGRN_DV), lambda b, h, c: (b, h, 0, 0))],
        out_shape=[jax.ShapeDtypeStruct((B * Tp, HGRN_W), f32),
                   jax.ShapeDtypeStruct((B, H, HGRN_DK, HGRN_DV), f32)],
        scratch_shapes=[pltpu.VMEM((HB, HGRN_DV, HGRN_DK), f32)] + [pltpu.VMEM((C, HGRN_DK), f32)] * 5,
        compiler_params=_params("parallel", "parallel", "arbitrary"),
        name="hgrn2_gla",
    )(src, src, src, src, lb_logits, norm_w.reshape(1, HGRN_DV), s0)


def _causal_conv_silu(x_ref, w_ref, b_ref, c0_ref, xp_sc, first, C):
    @pl.when(first)
    def _():
        xp_sc[0:8, :] = c0_ref[0]

    xp_sc[8:8 + C, :] = x_ref[...]
    w = w_ref[...]
    acc = b_ref[...] + w[0:1, :] * xp_sc[pl.ds(8 - (SSD_CONV - 1), C), :]
    for i in range(1, SSD_CONV):
        acc = acc + w[i:i + 1, :] * xp_sc[pl.ds(8 - (SSD_CONV - 1) + i, C), :]
    xp_sc[0:8, :] = xp_sc[C:C + 8, :]
    return jax.nn.silu(acc)


def _ssd_body(z_ref, x_ref, bm_ref, cm_ref, dt_ref, wx_ref, wb_ref, wc_ref, bx_ref, bb_ref, bc_ref,
              c0x_ref, c0b_ref, c0c_ref, dtb_ref, alog_ref, dsk_ref, nw_ref, s0_ref,
              y_ref, so_ref, st_sc, xpx_sc, xpb_sc, xpc_sc, *, C, t_valid):
    g = pl.program_id(1)
    c = pl.program_id(2)
    R = SSD_R
    npair = SSD_GW // LANES

    @pl.when(c == 0)
    def _():
        s0 = s0_ref[0, 0]
        for i in range(npair):
            st_sc[:, i * LANES:(i + 1) * LANES] = s0[i * LANES:(i + 1) * LANES, :].T

    xs = _causal_conv_silu(x_ref, wx_ref, bx_ref, c0x_ref, xpx_sc, c == 0, C)
    bm = _causal_conv_silu(bm_ref, wb_ref, bb_ref, c0b_ref, xpb_sc, c == 0, C).astype(bf16)
    cm = _causal_conv_silu(cm_ref, wc_ref, bc_ref, c0c_ref, xpc_sc, c == 0, C).astype(bf16)

    shift = (LANES - g * R) % LANES
    dt_all = jax.nn.softplus(dt_ref[...] + dtb_ref[0:1, :])
    valid = (c * C + lax.broadcasted_iota(i32, (C, 1), 0)) < t_valid
    dt_all = jnp.where(valid, dt_all, 0.0)
    la = dt_all * (-jnp.exp(alog_ref[0:1, :]))
    cs_all = _cumsum_axis(la, 0)
    cs = pltpu.roll(cs_all, shift, axis=1)
    dt = pltpu.roll(dt_all, shift, axis=1)
    dsk = pltpu.roll(dsk_ref[...], shift, axis=1)[0:1, :]
    cs_t = cs.T
    dt_t = dt.T
    cl = cs[C - 1:C, :]
    ecs = jnp.exp(cs)
    wst = dt * jnp.exp(cl - cs)
    ecl = jnp.exp(cl)

    cb = lax.dot_general(cm, bm, _NT, preferred_element_type=f32)
    causal = lax.broadcasted_iota(i32, (C, C), 0) >= lax.broadcasted_iota(i32, (C, C), 1)
    low = lax.broadcasted_iota(i32, (C, LANES), 1) < SSD_HD

    def expand(a):
        lw = low[0:a.shape[0], :]
        return jnp.concatenate(
            [jnp.where(lw, a[:, 2 * p:2 * p + 1], a[:, 2 * p + 1:2 * p + 2]) for p in range(npair)], axis=1)

    ys = []
    for p in range(npair):
        xp = xs[:, p * LANES:(p + 1) * LANES]
        acc = None
        for hh in range(2):
            r = 2 * p + hh
            dec = jnp.exp(jnp.where(causal, cs[:, r:r + 1] - cs_t[r:r + 1, :], -jnp.inf))
            wm = (cb * dec * dt_t[r:r + 1, :]).astype(bf16)
            xm = jnp.where(low if hh == 0 else jnp.logical_not(low), xp, 0.0).astype(bf16)
            part = jnp.dot(wm, xm, preferred_element_type=f32)
            acc = part if acc is None else acc + part
        ys.append(acc)
    y = jnp.concatenate(ys, axis=1)

    st = st_sc[...]
    y = y + jnp.dot(cm, st.astype(bf16), preferred_element_type=f32) * expand(ecs)
    y = y + expand(dsk) * xs

    xw = (xs * expand(wst)).astype(bf16)
    st_new = st * expand(ecl) + lax.dot_general(bm, xw, _TN, preferred_element_type=f32)
    st_sc[...] = st_new

    yg = y * jax.nn.silu(z_ref[...])
    y_ref[...] = _rms(yg, nw_ref[...]).astype(y_ref.dtype)

    @pl.when(c == pl.num_programs(2) - 1)
    def _():
        for i in range(npair):
            so_ref[0, 0, i * LANES:(i + 1) * LANES, :] = st_new[:, i * LANES:(i + 1) * LANES].T


def ssd(src, conv_w, conv_b, conv0_p, dt_bias8, a_log8, d_skip8, norm_w, s0, B, Tp, t_valid):
    G, C = SSD_GROUPS, SEQ_CHUNK
    nc = Tp // C
    GW, N = SSD_GW, SSD_N
    dt_blk = (SSD_INNER + SSD_CONV_DIM) // LANES
    x0 = SSD_INNER // GW
    b0 = 2 * SSD_INNER // N
    wb0 = SSD_INNER // N
    body = functools.partial(_ssd_body, C=C, t_valid=t_valid)
    par = lambda b, g, c: (0, 0)
    row = lambda off: (lambda b, g, c: (b * nc + c, off + g))
    wcol = lambda off: (lambda b, g, c: (0, off + g))
    c0col = lambda off: (lambda b, g, c: (b, 0, off + g))
    return pl.pallas_call(
        body,
        grid=(B, G, nc),
        in_specs=[pl.BlockSpec((C, GW), row(0)),
                  pl.BlockSpec((C, GW), row(x0)),
                  pl.BlockSpec((C, N), row(b0)),
                  pl.BlockSpec((C, N), row(b0 + G)),
                  pl.BlockSpec((C, LANES), lambda b, g, c: (b * nc + c, dt_blk)),
                  pl.BlockSpec((SSD_CONV, GW), wcol(0)),
                  pl.BlockSpec((SSD_CONV, N), wcol(wb0)),
                  pl.BlockSpec((SSD_CONV, N), wcol(wb0 + G)),
                  pl.BlockSpec((1, GW), wcol(0)),
                  pl.BlockSpec((1, N), wcol(wb0)),
                  pl.BlockSpec((1, N), wcol(wb0 + G)),
                  pl.BlockSpec((1, 8, GW), c0col(0)),
                  pl.BlockSpec((1, 8, N), c0col(wb0)),
                  pl.BlockSpec((1, 8, N), c0col(wb0 + G)),
                  pl.BlockSpec((8, LANES), par), pl.BlockSpec((8, LANES), par), pl.BlockSpec((8, LANES), par),
                  pl.BlockSpec((1, GW), lambda b, g, c: (0, g)),
                  pl.BlockSpec((1, 1, GW, N), lambda b, g, c: (b, g, 0, 0))],
        out_specs=[pl.BlockSpec((C, GW), lambda b, g, c: (b * nc + c, g)),
                   pl.BlockSpec((1, 1, GW, N), lambda b, g, c: (b, g, 0, 0))],
        out_shape=[jax.ShapeDtypeStruct((B * Tp, SSD_INNER), bf16),
                   jax.ShapeDtypeStruct((B, G, GW, N), f32)],
        scratch_shapes=[pltpu.VMEM((N, GW), f32), pltpu.VMEM((C + 8, GW), f32), pltpu.VMEM((C + 8, N), f32),
                        pltpu.VMEM((C + 8, N), f32)],
        compiler_params=_params("parallel", "parallel", "arbitrary"),
        name="ssd_scan",
    )(src, src, src, src, src, conv_w, conv_w, conv_w, conv_b.reshape(1, SSD_CONV_DIM),
      conv_b.reshape(1, SSD_CONV_DIM), conv_b.reshape(1, SSD_CONV_DIM), conv0_p, conv0_p, conv0_p,
      dt_bias8, a_log8, d_skip8, norm_w.reshape(1, SSD_INNER), s0)


def _sort_network(n):
    size = 1
    while size < n:
        size *= 2
    pairs = []
    p = 1
    while p < size:
        k = p
        while k >= 1:
            for j in range(k % p, size - k, 2 * k):
                for i in range(min(k, size - j - k)):
                    if (i + j) // (2 * p) == (i + j + k) // (2 * p):
                        pairs.append((i + j, i + j + k))
            k //= 2
        p *= 2
    return [(a, b) for a, b in pairs if b < n]


def _topk_sorted(s, k):
    n_rows, n = s.shape
    sl = 8
    nv = n_rows // sl
    vals = [s[sl * v:sl * (v + 1), :] for v in range(nv)]
    ids = [jnp.full((sl, n), v, i32) for v in range(nv)]
    for a, b in _sort_network(nv):
        va, vb, ia, ib = vals[a], vals[b], ids[a], ids[b]
        swap = (vb > va) | ((vb == va) & (ib < ia))
        vals[a], vals[b] = jnp.where(swap, vb, va), jnp.where(swap, va, vb)
        ids[a], ids[b] = jnp.where(swap, ib, ia), jnp.where(swap, ia, ib)
    sub = lax.broadcasted_iota(i32, (sl, n), 0)
    kio = lax.broadcasted_iota(i32, (k, n), 0)
    out_v = jnp.zeros((k, n), f32)
    out_i = jnp.zeros((k, n), i32)
    neg = jnp.full((sl, n), -jnp.inf, f32)
    for it in range(k):
        rid = ids[0] * sl + sub
        m = jnp.max(vals[0], axis=0, keepdims=True)
        idx = jnp.min(jnp.where(vals[0] == m, rid, n_rows), axis=0, keepdims=True)
        out_v = jnp.where(kio == it, m, out_v)
        out_i = jnp.where(kio == it, idx, out_i)
        sel = rid == idx
        for p in range(min(nv, k - 1 - it)):
            nxt_v = vals[p + 1] if p + 1 < nv else neg
            nxt_i = ids[p + 1] if p + 1 < nv else ids[p]
            vals[p] = jnp.where(sel, nxt_v, vals[p])
            ids[p] = jnp.where(sel, nxt_i, ids[p])
    return out_v, out_i


_CAND_HALF = PEER_TOPK // 2


def _cand_rows(v0, v1):
    K, Hh = PEER_TOPK, _CAND_HALF
    return jnp.concatenate([v0[0:1, :] + v1] + [v0[p:p + 1, :] + v1[0:Hh, :] for p in range(1, Hh)]
                           + [v0[Hh:K, :] + v1[0:1, :]], axis=0)


def _cand_pq(pos):
    K, Hh = PEER_TOPK, _CAND_HALF
    mid = pos - K
    tail = K + Hh * (Hh - 1)
    p = jnp.where(pos < K, 0, jnp.where(pos < tail, 1 + mid // Hh, Hh + (pos - tail)))
    q = jnp.where(pos < K, pos, jnp.where(pos < tail, mid % Hh, 0))
    return p, q


def _peer_route_body(x_ref, nw_ref, wqt_ref, keys_ref, xn_ref, i1_ref, i2_ref, g_ref,
                     qt_sc, i1t_sc, i2t_sc, gt_sc, *, tm):
    K = PEER_TOPK
    xn = _rms(x_ref[...], nw_ref[...]).astype(bf16)
    xn_ref[...] = xn
    qt_sc[...] = lax.dot_general(wqt_ref[...], xn, _NT, preferred_element_type=f32)

    def head(h, carry):
        base = pl.multiple_of(h * PEER_DQ, PEER_DQ)
        rows = pl.ds(pl.multiple_of(h * K, K), K)
        for half in range(tm // LANES):
            cols = slice(half * LANES, (half + 1) * LANES)
            sts = []
            for part in range(2):
                qp = qt_sc[pl.ds(base + part * (PEER_DQ // 2), PEER_DQ // 2), cols].astype(bf16)
                sts.append(jnp.dot(keys_ref[h, part], qp, preferred_element_type=f32))
            v0, i0 = _topk_sorted(sts[0], K)
            v1, i1 = _topk_sorted(sts[1], K)
            top, pos = _topk_sorted(_cand_rows(v0, v1), K)
            pp, qq = _cand_pq(pos)
            i1s = jnp.zeros((K, LANES), i32)
            i2s = jnp.zeros((K, LANES), i32)
            for r in range(K):
                i1s = jnp.where(pp == r, i0[r:r + 1, :], i1s)
                i2s = jnp.where(qq == r, i1[r:r + 1, :], i2s)
            e = jnp.exp(top - jnp.max(top, axis=0, keepdims=True))
            i1t_sc[rows, cols] = i1s
            i2t_sc[rows, cols] = i2s
            gt_sc[rows, cols] = e / jnp.sum(e, axis=0, keepdims=True)
        return carry

    lax.fori_loop(0, PEER_HEADS, head, 0)
    i1_ref[...] = i1t_sc[...].T
    i2_ref[...] = i2t_sc[...].T
    g_ref[...] = gt_sc[...].T


def peer_route(x, nw, wq_t, keys):
    M, D = x.shape
    tm = 256 if M % 256 == 0 else LANES
    HQ = PEER_HEADS * PEER_DQ
    S = PEER_SLOTS
    body = functools.partial(_peer_route_body, tm=tm)
    return pl.pallas_call(
        body,
        grid=(M // tm,),
        in_specs=[pl.BlockSpec((tm, D), lambda i: (i, 0)),
                  pl.BlockSpec((1, D), lambda i: (0, 0)),
                  pl.BlockSpec((HQ, D), lambda i: (0, 0)),
                  pl.BlockSpec((PEER_HEADS, 2, PEER_NKEYS, PEER_DQ // 2), lambda i: (0, 0, 0, 0))],
        out_specs=[pl.BlockSpec((tm, D), lambda i: (i, 0)),
                   pl.BlockSpec((tm, S), lambda i: (i, 0)),
                   pl.BlockSpec((tm, S), lambda i: (i, 0)),
                   pl.BlockSpec((tm, S), lambda i: (i, 0))],
        out_shape=[jax.ShapeDtypeStruct((M, D), bf16), jax.ShapeDtypeStruct((M, S), i32),
                   jax.ShapeDtypeStruct((M, S), i32), jax.ShapeDtypeStruct((M, S), f32)],
        scratch_shapes=[pltpu.VMEM((HQ, tm), f32), pltpu.VMEM((S, tm), i32), pltpu.VMEM((S, tm), i32),
                        pltpu.VMEM((S, tm), f32)],
        compiler_params=_params("parallel"),
        name="peer_route",
    )(x, nw.reshape(1, D), wq_t, keys)


def _peer_hidden_body(xn_ref, u_ref, i1_ref, i2_ref, g_ref, c_ref, h_sc, s_sc, *, te):
    e = pl.program_id(1)

    @pl.when(e == 0)
    def _():
        h_sc[...] = jnp.zeros(h_sc.shape, f32)
        s_sc[1] = jnp.zeros(s_sc.shape[1:], f32)

    slot = e % 2
    NK = PEER_NKEYS
    nb = te // NK
    i1 = i1_ref[...]
    i2 = i2_ref[...]
    h = h_sc[...]
    for sub in range(nb):
        got = jnp.take_along_axis(s_sc[1 - slot, :, sub * NK:(sub + 1) * NK], i2, axis=1)
        h = jnp.where(i1 == (e - 1) * nb + sub, got, h)
    h_sc[...] = h
    s_sc[slot] = lax.dot_general(xn_ref[...], u_ref[...], _NT, preferred_element_type=f32)

    @pl.when(e == pl.num_programs(1) - 1)
    def _():
        act = 0.5 * h * (1.0 + lax.erf(h * (2.0 ** -0.5)))
        c_ref[...] = g_ref[...] * act


def peer_hidden(xn, u, i1, i2, g):
    M, D = xn.shape
    E = u.shape[0]
    S = PEER_SLOTS
    tm, te = _tile(M, 1024), 512
    ne = E // te
    tok = lambda i, e: (i, 0)
    body = functools.partial(_peer_hidden_body, te=te)
    return pl.pallas_call(
        body,
        grid=(M // tm, ne + 1),
        in_specs=[pl.BlockSpec((tm, D), tok),
                  pl.BlockSpec((te, D), lambda i, e: (jnp.minimum(e, ne - 1), 0)),
                  pl.BlockSpec((tm, S), tok), pl.BlockSpec((tm, S), tok), pl.BlockSpec((tm, S), tok)],
        out_specs=pl.BlockSpec((tm, S), tok),
        out_shape=jax.ShapeDtypeStruct((M, S), f32),
        scratch_shapes=[pltpu.VMEM((tm, S), f32), pltpu.VMEM((2, tm, te), f32)],
        compiler_params=_params("parallel", "arbitrary"),
        name="peer_hidden",
    )(xn, u, i1, i2, g)


G_PITCH = PEER_NKEYS + 4
G_UNROLL = 16


def _peer_out_body(i1_ref, i2_ref, c_ref, v_ref, x_ref, o_ref, g_sc, *, tm, te):
    e = pl.program_id(1)
    NK = PEER_NKEYS

    @pl.when(e == 0)
    def _():
        o_ref[...] = x_ref[...]
        sub = lax.broadcasted_iota(i32, (NK, PEER_SLOTS), 0)

        def tok_group(gi, carry):
            for u in range(G_UNROLL):
                n = gi * G_UNROLL + u
                r1 = i1_ref[pl.ds(n, 1), :]
                r2 = i2_ref[pl.ds(n, 1), :]
                rc = c_ref[pl.ds(n, 1), :]
                at = jnp.where(sub == r1, rc, 0.0).astype(bf16)
                bt = jnp.where(sub == r2, 1.0, 0.0).astype(bf16)
                g_sc[pl.ds(n * G_PITCH, NK), :] = lax.dot_general(at, bt, _NT, preferred_element_type=f32)
            return carry

        lax.fori_loop(0, tm // G_UNROLL, tok_group, 0)

    nb = te // NK
    gcat = jnp.concatenate([g_sc[pl.ds(e * nb + s, tm, stride=G_PITCH), :] for s in range(nb)],
                           axis=1).astype(bf16)
    o_ref[...] += jnp.dot(gcat, v_ref[...], preferred_element_type=f32)


def peer_out(i1, i2, c, v, x):
    M, D = x.shape
    E = v.shape[0]
    S = PEER_SLOTS
    tm, te = _tile(M, 256), 1024
    tok = lambda i, e: (i, 0)
    body = functools.partial(_peer_out_body, tm=tm, te=te)
    return pl.pallas_call(
        body,
        grid=(M // tm, E // te),
        in_specs=[pl.BlockSpec((tm, S), tok), pl.BlockSpec((tm, S), tok), pl.BlockSpec((tm, S), tok),
                  pl.BlockSpec((te, D), lambda i, e: (e, 0)),
                  pl.BlockSpec((tm, D), tok)],
        out_specs=pl.BlockSpec((tm, D), tok),
        out_shape=jax.ShapeDtypeStruct((M, D), f32),
        scratch_shapes=[pltpu.VMEM((tm * G_PITCH, PEER_NKEYS), f32)],
        compiler_params=_params("parallel", "arbitrary"),
        name="peer_out",
    )(i1, i2, c, v, x)


def peer_ffn(x, nw, wq_t, keys, u, v):
    M, D = x.shape
    Mp = -(-M // LANES) * LANES
    xp = jnp.pad(x, ((0, Mp - M), (0, 0))) if Mp != M else x
    xn, i1, i2, g = peer_route(xp, nw, wq_t, keys)
    c = peer_hidden(xn, u, i1, i2, g)
    out = peer_out(i1, i2, c, v, xp)
    return out[:M] if Mp != M else out


def _pad_time(a, B, T, Tp):
    if Tp == T:
        return a
    W = a.shape[1]
    return jnp.pad(a.reshape(B, T, W), ((0, 0), (0, Tp - T), (0, 0))).reshape(B * Tp, W)


def _unpad_time(a, B, T, Tp):
    if Tp == T:
        return a
    W = a.shape[1]
    return a.reshape(B, Tp, W)[:, :T].reshape(B * T, W)


def _trunk(x, past, hgrn0, ssm0, conv0, wts):
    B, T, D = x.shape
    M = B * T
    H = FOX_HEADS
    if T % SEQ_CHUNK == 0:
        Tp = T
    else:
        assert T <= SEQ_CHUNK
        Tp = SEQ_CHUNK
    x2 = x.reshape(M, D)

    proj = norm_matmul(x2, wts["norm_mix"][0], wts["w_in_a"], "in_proj_a")
    ff = proj[:, 7 * FOX_W:7 * FOX_W + H]
    fk = proj[:, FOX_W:2 * FOX_W].reshape(B, T, H, FOX_HD)
    fv = proj[:, 2 * FOX_W:3 * FOX_W].reshape(B, T, H, FOX_HD)
    if past is None:
        ff_t = ff.reshape(B, T, H).transpose(0, 2, 1).reshape(B * H, T)
        lf_t, c_t = fox_gate(ff_t, jnp.tile(wts["b_fox_f"], B).reshape(B * H, 1))
        flogf = lf_t.reshape(B, H, T).transpose(0, 2, 1)
        c_bth = c_t.reshape(B, H, T).transpose(0, 2, 1)
        fo = fox_prompt_attn(proj, c_bth, c_t.reshape(B * H, 1, T), B, T)
    else:
        assert T * H <= LANES
        k_pool, v_pool, lf_pool, page_table = past
        n_phys = k_pool.shape[0]
        pad = LANES - T * H
        ff_l = jnp.pad(ff.reshape(B, T * H), ((0, 0), (0, pad)))
        lf_l, c_l = fox_gate(ff_l, jnp.tile(wts["b_fox_f"], LANES // H).reshape(1, LANES), stride=H)
        flogf = lf_l[:, :T * H].reshape(B, T, H)
        sfx_pool, tot_pool = page_suffix_sums(lf_pool.reshape(n_phys, PAGE * H), H)
        q_r = proj[:, 0:FOX_W].reshape(B, T * H, FOX_HD).astype(bf16)
        k_new = jnp.pad(proj[:, FOX_W:2 * FOX_W].reshape(B, T * H, FOX_HD), ((0, 0), (0, pad), (0, 0)))
        v_new = jnp.pad(proj[:, 2 * FOX_W:3 * FOX_W].reshape(B, T * H, FOX_HD), ((0, 0), (0, pad), (0, 0)))
        fo = fox_decode_attn(q_r, k_new, v_new, c_l.reshape(B, 1, LANES),
                             k_pool.reshape(n_phys, PAGE * H, FOX_HD), v_pool.reshape(n_phys, PAGE * H, FOX_HD),
                             sfx_pool.reshape(n_phys, 1, PAGE * H), tot_pool.reshape(n_phys, 1, PAGE * H),
                             page_table).reshape(M, FOX_W)

    if Tp == T:
        hsrc, hcb0 = proj, 3 * H
    else:
        hsrc, hcb0 = _pad_time(proj[:, 3 * FOX_W:3 * FOX_W + 4 * HGRN_W], B, T, Tp), 0
    ho, h_state = gla(hsrc, hcb0, wts["hgrn_lb_logits"], wts["hgrn_norm_w"], hgrn0, B, Tp, T, 0)
    ho = _unpad_time(ho, B, T, Tp)
    x2 = matmul_residual(jnp.concatenate([fo, ho], axis=1), wts["w_out_a"], x2, "out_proj_a")
    x2 = peer_ffn(x2, wts["norm_ffn"][0], wts["peer_wq_t"][0], wts["peer_keys"][0], wts["peer_u"][0],
                  wts["peer_v"][0])

    projc = norm_matmul(x2, wts["norm_mix"][1], wts["w_in_c"], "in_proj_c")
    xbc_raw = projc[:, SSD_INNER:SSD_INNER + SSD_CONV_DIM].reshape(B, T, SSD_CONV_DIM)
    if T >= SSD_CONV - 1:
        conv_state = xbc_raw[:, T - (SSD_CONV - 1):]
    else:
        conv_state = jnp.concatenate([conv0, xbc_raw], axis=1)[:, -(SSD_CONV - 1):]
    csrc = _pad_time(projc, B, T, Tp)
    conv0_p = jnp.pad(conv0, ((0, 0), (8 - (SSD_CONV - 1), 0), (0, 0)))
    yg, s_state = ssd(csrc, wts["conv_w"], wts["conv_b"], conv0_p, wts["dt_bias8"], wts["a_log8"],
                      wts["d_skip8"], wts["ssd_norm_w"], ssm0.reshape(B, SSD_GROUPS, SSD_GW, SSD_N), B, Tp, T)
    x2 = matmul_residual(_unpad_time(yg, B, T, Tp), wts["w_out_c"], x2, "out_proj_c")
    x2 = peer_ffn(x2, wts["norm_ffn"][1], wts["peer_wq_t"][1], wts["peer_keys"][1], wts["peer_u"][1],
                  wts["peer_v"][1])

    y = final_norm(x2, wts["norm_final"]).reshape(B, T, D)
    return (y, fk[None], fv[None], flogf[None], h_state[None],
            s_state.reshape(B, SSD_HEADS, SSD_HD, SSD_N)[None], conv_state[None])


def _row8(a):
    return jnp.tile(jnp.pad(a.astype(f32), (0, LANES - a.shape[0]))[None, :], (8, 1))


def kernel(x_prompt, x_sample, cache_fox_k, cache_fox_v, cache_fox_logf, state_hgrn, state_ssm, state_conv, page_table, norm_mix, norm_ffn, norm_final, w_in_a, b_fox_f, hgrn_lb_logits, hgrn_norm_w, w_out_a, w_in_c, conv_w, conv_b, dt_bias, a_log, d_skip, ssd_norm_w, w_out_c, peer_wq, peer_keys, peer_u, peer_v):
    D = x_prompt.shape[-1]
    wa = w_in_a[0]
    n_main = 3 * FOX_W
    wa_r = jnp.concatenate(
        [wa[:, :n_main], wa[:, n_main + FOX_HEADS:],
         jnp.pad(wa[:, n_main:n_main + FOX_HEADS], ((0, 0), (0, N_TAIL - FOX_HEADS)))], axis=1).astype(bf16)
    wc = w_in_c[0]
    wc_r = jnp.pad(wc, ((0, 0), (0, N_TAIL - SSD_HEADS))).astype(bf16)
    wts = dict(
        norm_mix=norm_mix, norm_ffn=norm_ffn, norm_final=norm_final,
        w_in_a=wa_r, b_fox_f=b_fox_f[0], hgrn_lb_logits=hgrn_lb_logits, hgrn_norm_w=hgrn_norm_w[0],
        w_out_a=w_out_a[0].astype(bf16),
        w_in_c=wc_r, conv_w=conv_w[0], conv_b=conv_b[0],
        dt_bias8=_row8(dt_bias[0]), a_log8=_row8(a_log[0]), d_skip8=_row8(d_skip[0]),
        ssd_norm_w=ssd_norm_w[0], w_out_c=w_out_c[0].astype(bf16),
        peer_wq_t=jnp.swapaxes(peer_wq, 1, 2).astype(bf16), peer_keys=peer_keys.astype(bf16),
        peer_u=peer_u.astype(bf16), peer_v=peer_v.astype(bf16),
    )
    Bp = x_prompt.shape[0]
    hgrn0_p = jnp.zeros((Bp, HGRN_HEADS, HGRN_DK, HGRN_DV), f32)
    ssm0_p = jnp.zeros((Bp, SSD_HEADS, SSD_HD, SSD_N), f32)
    conv0_p = jnp.zeros((Bp, SSD_CONV - 1, SSD_CONV_DIM), f32)
    y_p, fk_p, fv_p, fl_p, h_p, s_p, c_p = _trunk(x_prompt, None, hgrn0_p, ssm0_p, conv0_p, wts)
    past = (cache_fox_k[0], cache_fox_v[0], cache_fox_logf[0], page_table)
    y_s, fk_s, fv_s, fl_s, h_s, s_s, c_s = _trunk(x_sample, past, state_hgrn[0], state_ssm[0], state_conv[0], wts)
    return (y_p, y_s, fk_p, fv_p, fl_p, fk_s, fv_s, fl_s, h_p, h_s, s_p, s_s, c_p, c_s)
```

```python
import functools
import math

import jax
import jax.numpy as jnp
from jax import lax
from jax.experimental import pallas as pl
from jax.experimental.pallas import tpu as pltpu

f32 = jnp.float32
bf16 = jnp.bfloat16
i32 = jnp.int32

EPS = 1e-6
LANES = 128
VMEM_LIMIT = 48 * 1024 * 1024

FOX_HD = 128
FOX_HEADS = 8
FOX_W = FOX_HEADS * FOX_HD
HGRN_HEADS = 8
HGRN_DK = 128
HGRN_DV = 128
HGRN_W = HGRN_HEADS * HGRN_DK
SSD_HD = 64
SSD_HEADS = 64
SSD_GROUPS = 8
SSD_R = SSD_HEADS // SSD_GROUPS
SSD_N = 128
SSD_INNER = SSD_HEADS * SSD_HD
SSD_GW = SSD_INNER // SSD_GROUPS
SSD_CONV = 4
SSD_CONV_DIM = SSD_INNER + 2 * SSD_GROUPS * SSD_N
PEER_HEADS = 8
PEER_NKEYS = 128
PEER_DQ = 256
PEER_TOPK = 16
PEER_SLOTS = PEER_HEADS * PEER_TOPK
PAGE = 128
SEQ_CHUNK = 128
DIAG = 16
GLA_SUB = 64
GLA_SAFE_LOG = 80.0
N_TAIL = 512

_NT = (((1,), (1,)), ((), ()))
_TN = (((0,), (0,)), ((), ()))


def _params(*sem):
    return pltpu.CompilerParams(dimension_semantics=sem, vmem_limit_bytes=VMEM_LIMIT)


def _tile(n, target):
    t = min(n, target)
    while n % t:
        t -= 8
    return t


def _cumsum_axis(x, axis, stride=1):
    n = x.shape[axis]
    idx = lax.broadcasted_iota(i32, x.shape, axis)
    s = stride
    while s < n:
        x = x + jnp.where(idx >= s, pltpu.roll(x, s, axis=axis), 0.0)
        s *= 2
    return x


def _row_reduce(x, combine, reduce):
    acc = x[:, 0:LANES]
    for i in range(1, x.shape[1] // LANES):
        acc = combine(acc, x[:, i * LANES:(i + 1) * LANES])
    return reduce(acc, axis=1, keepdims=True)


def _rms(x, w):
    ms = jnp.mean(x * x, axis=-1, keepdims=True)
    return x * lax.rsqrt(ms + EPS) * w


def _norm_matmul_body(x_ref, nw_ref, w_ref, o_ref, xn_ref):
    @pl.when(pl.program_id(1) == 0)
    def _():
        xn_ref[...] = _rms(x_ref[...], nw_ref[...]).astype(bf16)

    o_ref[...] = jnp.dot(xn_ref[...], w_ref[...], preferred_element_type=f32)


def norm_matmul(x, nw, w, name):
    M, K = x.shape
    N = w.shape[1]
    tm, tn = _tile(M, 1024), _tile(N, 768)
    return pl.pallas_call(
        _norm_matmul_body,
        grid=(M // tm, N // tn),
        in_specs=[pl.BlockSpec((tm, K), lambda i, j: (i, 0)),
                  pl.BlockSpec((1, K), lambda i, j: (0, 0)),
                  pl.BlockSpec((K, tn), lambda i, j: (0, j))],
        out_specs=pl.BlockSpec((tm, tn), lambda i, j: (i, j)),
        out_shape=jax.ShapeDtypeStruct((M, N), f32),
        scratch_shapes=[pltpu.VMEM((tm, K), bf16)],
        compiler_params=_params("parallel", "arbitrary"),
        name=name,
    )(x, nw.reshape(1, K), w)


def _matmul_res_body(*refs, n_parts):
    a_refs, w_ref, r_ref, o_ref = refs[:n_parts], refs[n_parts], refs[n_parts + 1], refs[n_parts + 2]
    acc = r_ref[...]
    k0 = 0
    for a_ref in a_refs:
        k1 = k0 + a_ref.shape[1]
        acc = acc + jnp.dot(a_ref[...].astype(bf16), w_ref[k0:k1, :], preferred_element_type=f32)
        k0 = k1
    o_ref[...] = acc


def matmul_residual(parts, w, res, name):
    M = parts[0].shape[0]
    K, N = w.shape
    tm, tn = _tile(M, 1024), _tile(N, 512)
    return pl.pallas_call(
        functools.partial(_matmul_res_body, n_parts=len(parts)),
        grid=(M // tm, N // tn),
        in_specs=([pl.BlockSpec((tm, a.shape[1]), lambda i, j: (i, 0)) for a in parts]
                  + [pl.BlockSpec((K, tn), lambda i, j: (0, j)),
                     pl.BlockSpec((tm, tn), lambda i, j: (i, j))]),
        out_specs=pl.BlockSpec((tm, tn), lambda i, j: (i, j)),
        out_shape=jax.ShapeDtypeStruct((M, N), f32),
        compiler_params=_params("parallel", "arbitrary"),
        name=name,
    )(*parts, w, res)


def _fox_gate_body(ff_ref, b_ref, lf_ref, c_ref, *, stride):
    lf = jax.nn.log_sigmoid(ff_ref[...] + b_ref[...])
    lf_ref[...] = lf
    c_ref[...] = _cumsum_axis(lf, 1, stride)


def fox_gate(ff, bias, stride=1):
    R, T = ff.shape
    shp = jax.ShapeDtypeStruct((R, T), f32)
    return pl.pallas_call(functools.partial(_fox_gate_body, stride=stride), out_shape=[shp, shp], name="fox_gate",
                          compiler_params=pltpu.CompilerParams(vmem_limit_bytes=VMEM_LIMIT))(ff, bias)


def _fox_attn_body(q_ref, k_ref, v_ref, ct_ref, cs_ref, o_ref, ko_ref, vo_ref, vt_sc, cs_sc, m_sc, l_sc, acc_sc,
                   *, tq, scale, hpb):
    hg = pl.program_id(1)
    qi = pl.program_id(2)
    D = FOX_HD
    T = k_ref.shape[0]

    @pl.when(qi == 0)
    def _():
        ko_ref[...] = k_ref[...]
        vo_ref[...] = v_ref[...]
        cs = cs_ref[0]
        lane = lax.broadcasted_iota(i32, cs.shape, 1)
        for j in range(hpb):
            col = jnp.sum(jnp.where(lane == hg * hpb + j, cs, 0.0), axis=1, keepdims=True)
            cs_sc[j] = jnp.broadcast_to(col, (T, LANES))
            for t in range(T // LANES):
                blk = v_ref[t * LANES:(t + 1) * LANES, j * D:(j + 1) * D]
                vt_sc[j, :, t * LANES:(t + 1) * LANES] = blk.T.astype(bf16)

    q0 = pl.multiple_of(qi * tq, tq)
    qs, cts = [], []
    for j in range(hpb):
        qs.append(q_ref[:, j * D:(j + 1) * D].astype(bf16))
        cts.append(ct_ref[j, :, pl.ds(q0, tq)])
        m_sc[j] = jnp.full((1, tq), -jnp.inf, f32)
        l_sc[j] = jnp.zeros((1, tq), f32)
        acc_sc[j] = jnp.zeros((D, tq), f32)
    causal = lax.broadcasted_iota(i32, (tq, tq), 0) <= lax.broadcasted_iota(i32, (tq, tq), 1)

    def block(ki, masked):
        start = pl.multiple_of(ki * tq, tq)
        for j in range(hpb):
            k = k_ref[pl.ds(start, tq), j * D:(j + 1) * D].astype(bf16)
            st = lax.dot_general(k, qs[j], _NT, preferred_element_type=f32) * scale
            csb = cs_sc[j, pl.ds(start, tq), :]
            st = st + cts[j] - jnp.concatenate([csb] * (tq // LANES), axis=1)
            if masked:
                st = jnp.where(causal, st, -jnp.inf)
            m_prev = m_sc[j]
            m_new = jnp.maximum(m_prev, jnp.max(st, axis=0, keepdims=True))
            alpha = jnp.exp(m_prev - m_new)
            p = jnp.exp(st - m_new)
            l_sc[j] = alpha * l_sc[j] + jnp.sum(p, axis=0, keepdims=True)
            acc_sc[j] = alpha * acc_sc[j] + jnp.dot(vt_sc[j, :, pl.ds(start, tq)], p.astype(bf16),
                                                    preferred_element_type=f32)
            m_sc[j] = m_new

    def body(ki, carry):
        block(ki, False)
        return carry

    lax.fori_loop(0, qi, body, 0)
    block(qi, True)
    for j in range(hpb):
        o_t = acc_sc[j] / l_sc[j]
        for t in range(tq // LANES):
            o_ref[t * LANES:(t + 1) * LANES, j * D:(j + 1) * D] = o_t[:, t * LANES:(t + 1) * LANES].T


def fox_prompt_attn(proj, c_bth, c_rows, B, T):
    tq = _tile(T, 256)
    nq = T // tq
    H = FOX_HEADS
    hpb = 4
    W = hpb * FOX_HD
    G = H // hpb
    body = functools.partial(_fox_attn_body, tq=tq, scale=FOX_HD ** -0.5, hpb=hpb)
    return pl.pallas_call(
        body,
        grid=(B, G, nq),
        in_specs=[pl.BlockSpec((tq, W), lambda b, g, qi: (b * nq + qi, g)),
                  pl.BlockSpec((T, W), lambda b, g, qi: (b, G + g)),
                  pl.BlockSpec((T, W), lambda b, g, qi: (b, 2 * G + g)),
                  pl.BlockSpec((hpb, 1, T), lambda b, g, qi: (b * G + g, 0, 0)),
                  pl.BlockSpec((1, T, H), lambda b, g, qi: (b, 0, 0))],
        out_specs=[pl.BlockSpec((tq, W), lambda b, g, qi: (b * nq + qi, g)),
                   pl.BlockSpec((T, W), lambda b, g, qi: (b, g)),
                   pl.BlockSpec((T, W), lambda b, g, qi: (b, g))],
        out_shape=[jax.ShapeDtypeStruct((B * T, FOX_W), f32)] * 3,
        scratch_shapes=[pltpu.VMEM((hpb, FOX_HD, T), bf16), pltpu.VMEM((hpb, T, LANES), f32),
                        pltpu.VMEM((hpb, 1, tq), f32), pltpu.VMEM((hpb, 1, tq), f32),
                        pltpu.VMEM((hpb, FOX_HD, tq), f32)],
        compiler_params=_params("parallel", "parallel", "arbitrary"),
        name="fox_prompt_attn",
    )(proj, proj, proj, c_rows, c_bth)


def _suffix_body(x_ref, sfx_ref, tot_ref, *, stride):
    x = x_ref[...]
    lane = lax.broadcasted_iota(i32, x.shape, 1)
    n = x.shape[1]
    y = jnp.where(lane < n - stride, pltpu.roll(x, n - stride, axis=1), 0.0)
    t = x
    s = stride
    while s < n:
        y = y + jnp.where(lane < n - s, pltpu.roll(y, n - s, axis=1), 0.0)
        t = t + pltpu.roll(t, s, axis=1)
        s *= 2
    sfx_ref[...] = y
    tot_ref[...] = t


def page_suffix_sums(lf_rows, stride):
    R, n = lf_rows.shape
    tr = _tile(R, 512)
    shp = jax.ShapeDtypeStruct((R, n), f32)
    return pl.pallas_call(
        functools.partial(_suffix_body, stride=stride),
        grid=(R // tr,),
        in_specs=[pl.BlockSpec((tr, n), lambda i: (i, 0))],
        out_specs=[pl.BlockSpec((tr, n), lambda i: (i, 0)), pl.BlockSpec((tr, n), lambda i: (i, 0))],
        out_shape=[shp, shp],
        compiler_params=_params("parallel"),
        name="fox_page_suffix",
    )(lf_rows)


def _fox_dec_body(pt_ref, q_ref, kn_ref, vn_ref, cn_ref, *refs, scale, G):
    del pt_ref
    kp_refs, vp_refs = refs[0:G], refs[G:2 * G]
    sfx_refs, tot_refs = refs[2 * G:3 * G], refs[3 * G:4 * G]
    o_ref, m_sc, l_sc, acc_sc, carry_sc, cc_sc = refs[4 * G:]
    H = FOX_HEADS
    j = pl.program_id(1)
    q = q_ref[0]

    def partial_softmax(k, v, bias):
        s = lax.dot_general(q, k.astype(bf16), _NT, preferred_element_type=f32) * scale + bias
        m = _row_reduce(s, jnp.maximum, jnp.max)
        p = jnp.exp(s - m)
        return m, _row_reduce(p, jnp.add, jnp.sum), jnp.dot(p.astype(bf16), v.astype(bf16),
                                                            preferred_element_type=f32)

    def merge(parts):
        m_prev = m_sc[...]
        m_new = m_prev
        for m, _, _ in parts:
            m_new = jnp.maximum(m_new, m)
        alpha = jnp.exp(m_prev - m_new)
        l = alpha * l_sc[...]
        acc = alpha * acc_sc[...]
        for m, lp, ap in parts:
            w = jnp.exp(m - m_new)
            l = l + w * lp
            acc = acc + w * ap
        m_sc[...] = m_new
        l_sc[...] = l
        acc_sc[...] = acc

    def step(k, v, bias):
        merge([partial_softmax(k, v, bias)])

    @pl.when(j == 0)
    def _():
        m_sc[...] = jnp.full(m_sc.shape, -jnp.inf, f32)
        l_sc[...] = jnp.zeros(l_sc.shape, f32)
        acc_sc[...] = jnp.zeros(acc_sc.shape, f32)
        carry_sc[...] = jnp.zeros(carry_sc.shape, f32)
        cn = cn_ref[0]
        shape = (q.shape[0], cn.shape[1])
        row = lax.broadcasted_iota(i32, shape, 0)
        col = lax.broadcasted_iota(i32, shape, 1)
        cc = jnp.sum(jnp.where(row == col, cn, 0.0), axis=1, keepdims=True)
        cc_sc[...] = cc
        ok = jnp.logical_and(col % H == row % H, col // H <= row // H)
        step(kn_ref[0], vn_ref[0], jnp.where(ok, cc - cn, -jnp.inf))

    @pl.when(j > 0)
    def _():
        carry = carry_sc[...]
        shape = (q.shape[0], carry.shape[1])
        same_head = lax.broadcasted_iota(i32, shape, 1) % H == lax.broadcasted_iota(i32, shape, 0) % H
        cc = cc_sc[...]
        parts = []
        for g in range(G):
            bias = jnp.where(same_head, cc + (carry + sfx_refs[g][0]), -jnp.inf)
            parts.append(partial_softmax(kp_refs[g][0], vp_refs[g][0], bias))
            carry = carry + tot_refs[g][0]
        merge(parts)
        carry_sc[...] = carry

    @pl.when(j == pl.num_programs(1) - 1)
    def _():
        o_ref[0] = acc_sc[...] / l_sc[...]


def fox_decode_attn(q, k_new, v_new, c_new, k_pool, v_pool, sfx_pool, tot_pool, page_table):
    Bd, n_pages = page_table.shape
    R, D = q.shape[1], q.shape[2]
    PW = k_pool.shape[1]
    G = math.gcd(n_pages, 8)

    def page(g):
        return lambda b, j, pt: (pt[b, n_pages - 1 - (jnp.maximum(j, 1) - 1) * G - g], 0, 0)

    same = lambda b, j, pt: (b, 0, 0)
    grid_spec = pltpu.PrefetchScalarGridSpec(
        num_scalar_prefetch=1,
        grid=(Bd, n_pages // G + 1),
        in_specs=([pl.BlockSpec((1, R, D), same),
                   pl.BlockSpec((1, LANES, D), same),
                   pl.BlockSpec((1, LANES, D), same),
                   pl.BlockSpec((1, 1, LANES), same)]
                  + [pl.BlockSpec((1, PW, D), page(g)) for g in range(G)]
                  + [pl.BlockSpec((1, PW, D), page(g)) for g in range(G)]
                  + [pl.BlockSpec((1, 1, PW), page(g)) for g in range(G)]
                  + [pl.BlockSpec((1, 1, PW), page(g)) for g in range(G)]),
        out_specs=pl.BlockSpec((1, R, D), same),
        scratch_shapes=[pltpu.VMEM((R, 1), f32), pltpu.VMEM((R, 1), f32), pltpu.VMEM((R, D), f32),
                        pltpu.VMEM((1, PW), f32), pltpu.VMEM((R, 1), f32)],
    )
    body = functools.partial(_fox_dec_body, scale=FOX_HD ** -0.5, G=G)
    return pl.pallas_call(
        body,
        grid_spec=grid_spec,
        out_shape=jax.ShapeDtypeStruct((Bd, R, D), f32),
        compiler_params=_params("parallel", "arbitrary"),
        name="fox_decode_attn",
    )(page_table, q, k_new, v_new, c_new, *([k_pool] * G), *([v_pool] * G), *([sfx_pool] * G),
      *([tot_pool] * G))


def _gla_body(q_ref, f_ref, i_ref, g_ref, lbl_ref, nw_ref, s0_ref, o_ref, so_ref,
              s_sc, q_sc, k_sc, b_sc, v_sc, oa_sc, *, C, t_valid, layer_j, HB):
    c = pl.program_id(2)
    SUB = GLA_SUB
    K = HGRN_DK

    @pl.when(c == 0)
    def _():
        for j in range(HB):
            s_sc[j] = s0_ref[0, j].T

    lg = lbl_ref[...]
    e = jnp.exp(lg - jnp.max(lg, axis=0, keepdims=True))
    sm = e / jnp.sum(e, axis=0, keepdims=True)
    lb = jnp.sum(sm[0:layer_j + 1], axis=0, keepdims=True)

    z = f_ref[...]
    hlogf = jnp.log(lb + (1.0 - lb) * jax.nn.sigmoid(z))
    hk = (1.0 - lb) * jax.nn.sigmoid(-z)
    valid = (c * C + lax.broadcasted_iota(i32, (C, 1), 0)) < t_valid
    lc = jnp.where(valid, hlogf, 0.0)
    kk = jnp.where(valid, hk, 0.0)
    subs = list(range(0, C, SUB))
    bs = [_cumsum_axis(lc[lo:lo + SUB, :], 0) for lo in subs]
    bl_min = bs[0][SUB - 1:SUB, :]
    for b_ in bs[1:]:
        bl_min = jnp.minimum(bl_min, b_[SUB - 1:SUB, :])
    in_range = jnp.min(bl_min) >= -GLA_SAFE_LOG
    tril = lax.broadcasted_iota(i32, (SUB, SUB), 1) <= lax.broadcasted_iota(i32, (SUB, SUB), 0)

    @pl.when(in_range)
    def _():
        for j in range(HB):
            cols = slice(j * K, (j + 1) * K)
            st = s_sc[j]
            for si, lo in enumerate(subs):
                rows = slice(lo, lo + SUB)
                b = bs[si][:, cols]
                bl = b[SUB - 1:SUB, :]
                ks = kk[rows, cols]
                vs = i_ref[rows, cols].astype(bf16)
                qd = (q_ref[rows, cols] * jnp.exp(b)).astype(bf16)
                ki = (ks * jnp.exp(-b)).astype(bf16)
                a = jnp.where(tril, lax.dot_general(qd, ki, _NT, preferred_element_type=f32), 0.0)
                o = (lax.dot_general(qd, st.astype(bf16), _NT, preferred_element_type=f32)
                     + jnp.dot(a.astype(bf16), vs, preferred_element_type=f32))
                kd = (ks * jnp.exp(bl - b)).astype(bf16)
                st = st * jnp.exp(bl) + lax.dot_general(vs, kd, _TN, preferred_element_type=f32)
                o_ref[rows, cols] = _rms(o, nw_ref[...]) * jax.nn.silu(g_ref[rows, cols])
            s_sc[j] = st

    def off_diag(lo, hi):
        if hi - lo <= DIAG:
            return
        mid = (lo + hi) // 2
        r = b_sc[pl.ds(mid - 1, 1), :]
        qh = q_sc[mid:hi, :] * jnp.exp(b_sc[mid:hi, :] - r)
        kh = k_sc[lo:mid, :] * jnp.exp(r - b_sc[lo:mid, :])
        a = lax.dot_general(qh.astype(bf16), kh.astype(bf16), _NT, preferred_element_type=f32)
        oa_sc[mid:hi, :] += jnp.dot(a.astype(bf16), v_sc[lo:mid, :].astype(bf16), preferred_element_type=f32)
        off_diag(lo, mid)
        off_diag(mid, hi)

    srow = lax.broadcasted_iota(i32, (DIAG, 1), 0)

    def diag(i, carry):
        lo = pl.multiple_of(i * DIAG, DIAG)
        qb = q_sc[pl.ds(lo, DIAG), :]
        kb = k_sc[pl.ds(lo, DIAG), :]
        bb = b_sc[pl.ds(lo, DIAG), :]
        vb = v_sc[pl.ds(lo, DIAG), :]
        ob = jnp.zeros((DIAG, vb.shape[1]), f32)
        for t in range(DIAG):
            dec = jnp.exp(jnp.where(srow <= t, bb[t:t + 1, :] - bb, -jnp.inf))
            w = jnp.sum(qb[t:t + 1, :] * kb * dec, axis=1, keepdims=True)
            ot = jnp.sum(w * vb, axis=0, keepdims=True)
            ob = jnp.where(srow == t, ot, ob)
        oa_sc[pl.ds(lo, DIAG), :] += ob
        return carry

    @pl.when(jnp.logical_not(in_range))
    def _():
        for j in range(HB):
            cols = slice(j * K, (j + 1) * K)
            q_sc[...] = q_ref[:, cols]
            k_sc[...] = kk[:, cols]
            v_sc[...] = i_ref[:, cols]
            for si, lo in enumerate(subs):
                rows = slice(lo, lo + SUB)
                b = bs[si][:, cols]
                b_sc[rows, :] = b
                bl = b[SUB - 1:SUB, :]
                st = s_sc[j]
                qd = (q_sc[rows, :] * jnp.exp(b)).astype(bf16)
                oa_sc[rows, :] = lax.dot_general(qd, st.astype(bf16), _NT, preferred_element_type=f32)
                off_diag(lo, lo + SUB)
                lax.fori_loop(lo // DIAG, (lo + SUB) // DIAG, diag, 0)
                kd = (k_sc[rows, :] * jnp.exp(bl - b)).astype(bf16)
                s_sc[j] = st * jnp.exp(bl) + lax.dot_general(v_sc[rows, :].astype(bf16), kd, _TN,
                                                             preferred_element_type=f32)
            o_ref[:, cols] = _rms(oa_sc[...], nw_ref[...]) * jax.nn.silu(g_ref[:, cols])

    @pl.when(c == pl.num_programs(2) - 1)
    def _():
        for j in range(HB):
            so_ref[0, j] = s_sc[j].T


def gla(src, cb0, lb_logits, norm_w, s0, B, Tp, t_valid, layer_j):
    H, C = HGRN_HEADS, SEQ_CHUNK
    HB = 4
    G = H // HB
    W = HB * HGRN_DK
    nc = Tp // C
    L = lb_logits.shape[0]
    assert cb0 % HB == 0

    def col(g):
        return pl.BlockSpec((C, W), lambda b, h, c: (b * nc + c, (cb0 + g * H) // HB + h))

    body = functools.partial(_gla_body, C=C, t_valid=t_valid, layer_j=layer_j, HB=HB)
    return pl.pallas_call(
        body,
        grid=(B, G, nc),
        in_specs=[col(0), col(1), col(2), col(3),
                  pl.BlockSpec((L, W), lambda b, h, c: (0, h)),
                  pl.BlockSpec((1, HGRN_DV), lambda b, h, c: (0, 0)),
                  pl.BlockSpec((1, HB, HGRN_DK, HGRN_DV), lambda b, h, c: (b, h, 0, 0))],
        out_specs=[pl.BlockSpec((C, W), lambda b, h, c: (b * nc + c, h)),
                   pl.BlockSpec((1, HB, HGRN_DK, HGRN_DV), lambda b, h, c: (b, h, 0, 0))],
        out_shape=[jax.ShapeDtypeStruct((B * Tp, HGRN_W), f32),
                   jax.ShapeDtypeStruct((B, H, HGRN_DK, HGRN_DV), f32)],
        scratch_shapes=[pltpu.VMEM((HB, HGRN_DV, HGRN_DK), f32)] + [pltpu.VMEM((C, HGRN_DK), f32)] * 5,
        compiler_params=_params("parallel", "parallel", "arbitrary"),
        name="hgrn2_gla",
    )(src, src, src, src, lb_logits, norm_w.reshape(1, HGRN_DV), s0)


def _causal_conv_silu(x_ref, w_ref, b_ref, c0_ref, xp_sc, first, C):
    @pl.when(first)
    def _():
        xp_sc[0:8, :] = c0_ref[0]

    xp_sc[8:8 + C, :] = x_ref[...]
    w = w_ref[...]
    acc = b_ref[...] + w[0:1, :] * xp_sc[pl.ds(8 - (SSD_CONV - 1), C), :]
    for i in range(1, SSD_CONV):
        acc = acc + w[i:i + 1, :] * xp_sc[pl.ds(8 - (SSD_CONV - 1) + i, C), :]
    xp_sc[0:8, :] = xp_sc[C:C + 8, :]
    return jax.nn.silu(acc)


def _ssd_body(z_ref, x_ref, bm_ref, cm_ref, dt_ref, wx_ref, wb_ref, wc_ref, bx_ref, bb_ref, bc_ref,
              c0x_ref, c0b_ref, c0c_ref, dtb_ref, alog_ref, dsk_ref, nw_ref, s0_ref,
              y_ref, so_ref, st_sc, xpx_sc, xpb_sc, xpc_sc, *, C, t_valid):
    g = pl.program_id(1)
    c = pl.program_id(2)
    R = SSD_R
    npair = SSD_GW // LANES

    @pl.when(c == 0)
    def _():
        s0 = s0_ref[0, 0]
        for i in range(npair):
            st_sc[:, i * LANES:(i + 1) * LANES] = s0[i * LANES:(i + 1) * LANES, :].T

    xs = _causal_conv_silu(x_ref, wx_ref, bx_ref, c0x_ref, xpx_sc, c == 0, C)
    bm = _causal_conv_silu(bm_ref, wb_ref, bb_ref, c0b_ref, xpb_sc, c == 0, C).astype(bf16)
    cm = _causal_conv_silu(cm_ref, wc_ref, bc_ref, c0c_ref, xpc_sc, c == 0, C).astype(bf16)

    shift = (LANES - g * R) % LANES
    dt_all = jax.nn.softplus(dt_ref[...] + dtb_ref[0:1, :])
    valid = (c * C + lax.broadcasted_iota(i32, (C, 1), 0)) < t_valid
    dt_all = jnp.where(valid, dt_all, 0.0)
    la = dt_all * (-jnp.exp(alog_ref[0:1, :]))
    cs_all = _cumsum_axis(la, 0)
    cs = pltpu.roll(cs_all, shift, axis=1)
    dt = pltpu.roll(dt_all, shift, axis=1)
    dsk = pltpu.roll(dsk_ref[...], shift, axis=1)[0:1, :]
    cs_t = cs.T
    dt_t = dt.T
    cl = cs[C - 1:C, :]
    ecs = jnp.exp(cs)
    wst = dt * jnp.exp(cl - cs)
    ecl = jnp.exp(cl)

    cb = lax.dot_general(cm, bm, _NT, preferred_element_type=f32)
    causal = lax.broadcasted_iota(i32, (C, C), 0) >= lax.broadcasted_iota(i32, (C, C), 1)
    low = lax.broadcasted_iota(i32, (C, LANES), 1) < SSD_HD

    def expand(a):
        lw = low[0:a.shape[0], :]
        return jnp.concatenate(
            [jnp.where(lw, a[:, 2 * p:2 * p + 1], a[:, 2 * p + 1:2 * p + 2]) for p in range(npair)], axis=1)

    ys = []
    for p in range(npair):
        xp = xs[:, p * LANES:(p + 1) * LANES]
        acc = None
        for hh in range(2):
            r = 2 * p + hh
            dec = jnp.exp(jnp.where(causal, cs[:, r:r + 1] - cs_t[r:r + 1, :], -jnp.inf))
            wm = (cb * dec * dt_t[r:r + 1, :]).astype(bf16)
            xm = jnp.where(low if hh == 0 else jnp.logical_not(low), xp, 0.0).astype(bf16)
            part = jnp.dot(wm, xm, preferred_element_type=f32)
            acc = part if acc is None else acc + part
        ys.append(acc)
    y = jnp.concatenate(ys, axis=1)

    st = st_sc[...]
    y = y + jnp.dot(cm, st.astype(bf16), preferred_element_type=f32) * expand(ecs)
    y = y + expand(dsk) * xs

    xw = (xs * expand(wst)).astype(bf16)
    st_new = st * expand(ecl) + lax.dot_general(bm, xw, _TN, preferred_element_type=f32)
    st_sc[...] = st_new

    yg = y * jax.nn.silu(z_ref[...])
    y_ref[...] = _rms(yg, nw_ref[...]).astype(y_ref.dtype)

    @pl.when(c == pl.num_programs(2) - 1)
    def _():
        for i in range(npair):
            so_ref[0, 0, i * LANES:(i + 1) * LANES, :] = st_new[:, i * LANES:(i + 1) * LANES].T


def ssd(src, conv_w, conv_b, conv0_p, dt_bias8, a_log8, d_skip8, norm_w, s0, B, Tp, t_valid):
    G, C = SSD_GROUPS, SEQ_CHUNK
    nc = Tp // C
    GW, N = SSD_GW, SSD_N
    dt_blk = (SSD_INNER + SSD_CONV_DIM) // LANES
    x0 = SSD_INNER // GW
    b0 = 2 * SSD_INNER // N
    wb0 = SSD_INNER // N
    body = functools.partial(_ssd_body, C=C, t_valid=t_valid)
    par = lambda b, g, c: (0, 0)
    row = lambda off: (lambda b, g, c: (b * nc + c, off + g))
    wcol = lambda off: (lambda b, g, c: (0, off + g))
    c0col = lambda off: (lambda b, g, c: (b, 0, off + g))
    return pl.pallas_call(
        body,
        grid=(B, G, nc),
        in_specs=[pl.BlockSpec((C, GW), row(0)),
                  pl.BlockSpec((C, GW), row(x0)),
                  pl.BlockSpec((C, N), row(b0)),
                  pl.BlockSpec((C, N), row(b0 + G)),
                  pl.BlockSpec((C, LANES), lambda b, g, c: (b * nc + c, dt_blk)),
                  pl.BlockSpec((SSD_CONV, GW), wcol(0)),
                  pl.BlockSpec((SSD_CONV, N), wcol(wb0)),
                  pl.BlockSpec((SSD_CONV, N), wcol(wb0 + G)),
                  pl.BlockSpec((1, GW), wcol(0)),
                  pl.BlockSpec((1, N), wcol(wb0)),
                  pl.BlockSpec((1, N), wcol(wb0 + G)),
                  pl.BlockSpec((1, 8, GW), c0col(0)),
                  pl.BlockSpec((1, 8, N), c0col(wb0)),
                  pl.BlockSpec((1, 8, N), c0col(wb0 + G)),
                  pl.BlockSpec((8, LANES), par), pl.BlockSpec((8, LANES), par), pl.BlockSpec((8, LANES), par),
                  pl.BlockSpec((1, GW), lambda b, g, c: (0, g)),
                  pl.BlockSpec((1, 1, GW, N), lambda b, g, c: (b, g, 0, 0))],
        out_specs=[pl.BlockSpec((C, GW), lambda b, g, c: (b * nc + c, g)),
                   pl.BlockSpec((1, 1, GW, N), lambda b, g, c: (b, g, 0, 0))],
        out_shape=[jax.ShapeDtypeStruct((B * Tp, SSD_INNER), bf16),
                   jax.ShapeDtypeStruct((B, G, GW, N), f32)],
        scratch_shapes=[pltpu.VMEM((N, GW), f32), pltpu.VMEM((C + 8, GW), f32), pltpu.VMEM((C + 8, N), f32),
                        pltpu.VMEM((C + 8, N), f32)],
        compiler_params=_params("parallel", "parallel", "arbitrary"),
        name="ssd_scan",
    )(src, src, src, src, src, conv_w, conv_w, conv_w, conv_b.reshape(1, SSD_CONV_DIM),
      conv_b.reshape(1, SSD_CONV_DIM), conv_b.reshape(1, SSD_CONV_DIM), conv0_p, conv0_p, conv0_p,
      dt_bias8, a_log8, d_skip8, norm_w.reshape(1, SSD_INNER), s0)


def _sort_network(n):
    size = 1
    while size < n:
        size *= 2
    pairs = []
    p = 1
    while p < size:
        k = p
        while k >= 1:
            for j in range(k % p, size - k, 2 * k):
                for i in range(min(k, size - j - k)):
                    if (i + j) // (2 * p) == (i + j + k) // (2 * p):
                        pairs.append((i + j, i + j + k))
            k //= 2
        p *= 2
    return [(a, b) for a, b in pairs if b < n]


def _topk_sorted(s, k):
    n_rows, n = s.shape
    sl = 8
    nv = n_rows // sl
    vals = [s[sl * v:sl * (v + 1), :] for v in range(nv)]
    ids = [jnp.full((sl, n), v, i32) for v in range(nv)]
    for a, b in _sort_network(nv):
        va, vb, ia, ib = vals[a], vals[b], ids[a], ids[b]
        swap = (vb > va) | ((vb == va) & (ib < ia))
        vals[a], vals[b] = jnp.where(swap, vb, va), jnp.where(swap, va, vb)
        ids[a], ids[b] = jnp.where(swap, ib, ia), jnp.where(swap, ia, ib)
    sub = lax.broadcasted_iota(i32, (sl, n), 0)
    kio = lax.broadcasted_iota(i32, (k, n), 0)
    out_v = jnp.zeros((k, n), f32)
    out_i = jnp.zeros((k, n), i32)
    neg = jnp.full((sl, n), -jnp.inf, f32)
    for it in range(k):
        rid = ids[0] * sl + sub
        m = jnp.max(vals[0], axis=0, keepdims=True)
        idx = jnp.min(jnp.where(vals[0] == m, rid, n_rows), axis=0, keepdims=True)
        out_v = jnp.where(kio == it, m, out_v)
        out_i = jnp.where(kio == it, idx, out_i)
        sel = rid == idx
        for p in range(min(nv, k - 1 - it)):
            nxt_v = vals[p + 1] if p + 1 < nv else neg
            nxt_i = ids[p + 1] if p + 1 < nv else ids[p]
            vals[p] = jnp.where(sel, nxt_v, vals[p])
            ids[p] = jnp.where(sel, nxt_i, ids[p])
    return out_v, out_i


_CAND_HALF = PEER_TOPK // 2


def _cand_rows(v0, v1):
    K, Hh = PEER_TOPK, _CAND_HALF
    return jnp.concatenate([v0[0:1, :] + v1] + [v0[p:p + 1, :] + v1[0:Hh, :] for p in range(1, Hh)]
                           + [v0[Hh:K, :] + v1[0:1, :]], axis=0)


def _cand_pq(pos):
    K, Hh = PEER_TOPK, _CAND_HALF
    mid = pos - K
    tail = K + Hh * (Hh - 1)
    p = jnp.where(pos < K, 0, jnp.where(pos < tail, 1 + mid // Hh, Hh + (pos - tail)))
    q = jnp.where(pos < K, pos, jnp.where(pos < tail, mid % Hh, 0))
    return p, q


def _peer_route_body(x_ref, nw_ref, wqt_ref, keys_ref, xn_ref, i1_ref, i2_ref, g_ref,
                     qt_sc, i1t_sc, i2t_sc, gt_sc, *, tm):
    K = PEER_TOPK
    xn = _rms(x_ref[...], nw_ref[...]).astype(bf16)
    xn_ref[...] = xn
    qt_sc[...] = lax.dot_general(wqt_ref[...], xn, _NT, preferred_element_type=f32)

    def head(h, carry):
        base = pl.multiple_of(h * PEER_DQ, PEER_DQ)
        rows = pl.ds(pl.multiple_of(h * K, K), K)
        for half in range(tm // LANES):
            cols = slice(half * LANES, (half + 1) * LANES)
            sts = []
            for part in range(2):
                qp = qt_sc[pl.ds(base + part * (PEER_DQ // 2), PEER_DQ // 2), cols].astype(bf16)
                sts.append(jnp.dot(keys_ref[h, part], qp, preferred_element_type=f32))
            v0, i0 = _topk_sorted(sts[0], K)
            v1, i1 = _topk_sorted(sts[1], K)
            top, pos = _topk_sorted(_cand_rows(v0, v1), K)
            pp, qq = _cand_pq(pos)
            i1s = jnp.zeros((K, LANES), i32)
            i2s = jnp.zeros((K, LANES), i32)
            for r in range(K):
                i1s = jnp.where(pp == r, i0[r:r + 1, :], i1s)
                i2s = jnp.where(qq == r, i1[r:r + 1, :], i2s)
            e = jnp.exp(top - jnp.max(top, axis=0, keepdims=True))
            i1t_sc[rows, cols] = i1s
            i2t_sc[rows, cols] = i2s
            gt_sc[rows, cols] = e / jnp.sum(e, axis=0, keepdims=True)
        return carry

    lax.fori_loop(0, PEER_HEADS, head, 0)
    i1_ref[...] = i1t_sc[...].T
    i2_ref[...] = i2t_sc[...].T
    g_ref[...] = gt_sc[...].T


def peer_route(x, nw, wq_t, keys):
    M, D = x.shape
    tm = 256 if M % 256 == 0 else LANES
    HQ = PEER_HEADS * PEER_DQ
    S = PEER_SLOTS
    body = functools.partial(_peer_route_body, tm=tm)
    return pl.pallas_call(
        body,
        grid=(M // tm,),
        in_specs=[pl.BlockSpec((tm, D), lambda i: (i, 0)),
                  pl.BlockSpec((1, D), lambda i: (0, 0)),
                  pl.BlockSpec((HQ, D), lambda i: (0, 0)),
                  pl.BlockSpec((PEER_HEADS, 2, PEER_NKEYS, PEER_DQ // 2), lambda i: (0, 0, 0, 0))],
        out_specs=[pl.BlockSpec((tm, D), lambda i: (i, 0)),
                   pl.BlockSpec((tm, S), lambda i: (i, 0)),
                   pl.BlockSpec((tm, S), lambda i: (i, 0)),
                   pl.BlockSpec((tm, S), lambda i: (i, 0))],
        out_shape=[jax.ShapeDtypeStruct((M, D), bf16), jax.ShapeDtypeStruct((M, S), i32),
                   jax.ShapeDtypeStruct((M, S), i32), jax.ShapeDtypeStruct((M, S), f32)],
        scratch_shapes=[pltpu.VMEM((HQ, tm), f32), pltpu.VMEM((S, tm), i32), pltpu.VMEM((S, tm), i32),
                        pltpu.VMEM((S, tm), f32)],
        compiler_params=_params("parallel"),
        name="peer_route",
    )(x, nw.reshape(1, D), wq_t, keys)


def _peer_hidden_body(xn_ref, u_ref, i1_ref, i2_ref, g_ref, c_ref, h_sc, s_sc, *, te):
    e = pl.program_id(1)

    @pl.when(e == 0)
    def _():
        h_sc[...] = jnp.zeros(h_sc.shape, f32)
        s_sc[1] = jnp.zeros(s_sc.shape[1:], f32)

    slot = e % 2
    NK = PEER_NKEYS
    nb = te // NK
    i1 = i1_ref[...]
    i2 = i2_ref[...]
    h = h_sc[...]
    for sub in range(nb):
        got = jnp.take_along_axis(s_sc[1 - slot, :, sub * NK:(sub + 1) * NK], i2, axis=1)
        h = jnp.where(i1 == (e - 1) * nb + sub, got, h)
    h_sc[...] = h
    s_sc[slot] = lax.dot_general(xn_ref[...], u_ref[...], _NT, preferred_element_type=f32)

    @pl.when(e == pl.num_programs(1) - 1)
    def _():
        act = 0.5 * h * (1.0 + lax.erf(h * (2.0 ** -0.5)))
        c_ref[...] = g_ref[...] * act


def peer_hidden(xn, u, i1, i2, g):
    M, D = xn.shape
    E = u.shape[0]
    S = PEER_SLOTS
    tm, te = _tile(M, 1024), 512
    ne = E // te
    tok = lambda i, e: (i, 0)
    body = functools.partial(_peer_hidden_body, te=te)
    return pl.pallas_call(
        body,
        grid=(M // tm, ne + 1),
        in_specs=[pl.BlockSpec((tm, D), tok),
                  pl.BlockSpec((te, D), lambda i, e: (jnp.minimum(e, ne - 1), 0)),
                  pl.BlockSpec((tm, S), tok), pl.BlockSpec((tm, S), tok), pl.BlockSpec((tm, S), tok)],
        out_specs=pl.BlockSpec((tm, S), tok),
        out_shape=jax.ShapeDtypeStruct((M, S), f32),
        scratch_shapes=[pltpu.VMEM((tm, S), f32), pltpu.VMEM((2, tm, te), f32)],
        compiler_params=_params("parallel", "arbitrary"),
        name="peer_hidden",
    )(xn, u, i1, i2, g)


G_HALF = PEER_NKEYS // 2
G_PITCH = G_HALF + 4
G_UNROLL = 16
_HI16 = 0xFFFF0000


def _bf16_bits_high(x):
    return pltpu.bitcast(x.astype(bf16).astype(f32), jnp.uint32)


def _peer_out_body(i1_ref, i2_ref, c_ref, va_ref, vb_ref, x_ref, nw_ref, o_ref, g_sc, *, tm, nr, final_norm):
    e = pl.program_id(1)
    NK = PEER_NKEYS

    @pl.when(e == 0)
    def _():
        o_ref[...] = x_ref[...]
        sub = lax.broadcasted_iota(i32, (NK, PEER_SLOTS), 0)

        def tok_group(gi, carry):
            for u in range(G_UNROLL):
                n = gi * G_UNROLL + u
                r1 = i1_ref[pl.ds(n, 1), :]
                r2 = i2_ref[pl.ds(n, 1), :]
                rc = c_ref[pl.ds(n, 1), :]
                at = jnp.where(sub == r1, rc, 0.0).astype(bf16)
                bt = jnp.where(sub == r2, 1.0, 0.0).astype(bf16)
                tile = lax.dot_general(at, bt, _NT, preferred_element_type=f32)
                word = _bf16_bits_high(tile[0:G_HALF, :]) | (_bf16_bits_high(tile[G_HALF:NK, :]) >> 16)
                g_sc[pl.ds(n * G_PITCH, G_HALF), :] = word
            return carry

        lax.fori_loop(0, tm // G_UNROLL, tok_group, 0)

    his, los = [], []
    for s in range(nr):
        w = g_sc[pl.ds(e * nr + s, tm, stride=G_PITCH), :]
        his.append(pltpu.bitcast(w & jnp.uint32(_HI16), f32).astype(bf16))
        los.append(pltpu.bitcast(w << 16, f32).astype(bf16))
    acc = o_ref[...] + jnp.dot(jnp.concatenate(his, axis=1), va_ref[...], preferred_element_type=f32)
    acc = acc + jnp.dot(jnp.concatenate(los, axis=1), vb_ref[...], preferred_element_type=f32)

    if final_norm:
        last = e == pl.num_programs(1) - 1

        @pl.when(last)
        def _():
            o_ref[...] = _rms(acc, nw_ref[...])

        @pl.when(jnp.logical_not(last))
        def _():
            o_ref[...] = acc
    else:
        o_ref[...] = acc


def peer_out(i1, i2, c, v, x, final_nw=None):
    M, D = x.shape
    E = v.shape[0]
    S = PEER_SLOTS
    tm = _tile(M, 512)
    nr = 4
    te = nr * PEER_NKEYS
    ne = G_HALF // nr
    assert E == PEER_NKEYS * PEER_NKEYS and G_HALF % nr == 0
    tok = lambda i, e: (i, 0)
    nw = jnp.ones((D,), f32) if final_nw is None else final_nw
    body = functools.partial(_peer_out_body, tm=tm, nr=nr, final_norm=final_nw is not None)
    return pl.pallas_call(
        body,
        grid=(M // tm, ne),
        in_specs=[pl.BlockSpec((tm, S), tok), pl.BlockSpec((tm, S), tok), pl.BlockSpec((tm, S), tok),
                  pl.BlockSpec((te, D), lambda i, e: (e, 0)),
                  pl.BlockSpec((te, D), lambda i, e: (ne + e, 0)),
                  pl.BlockSpec((tm, D), tok),
                  pl.BlockSpec((1, D), lambda i, e: (0, 0))],
        out_specs=pl.BlockSpec((tm, D), tok),
        out_shape=jax.ShapeDtypeStruct((M, D), f32),
        scratch_shapes=[pltpu.VMEM((tm * G_PITCH, PEER_NKEYS), jnp.uint32)],
        compiler_params=_params("parallel", "arbitrary"),
        name="peer_out",
    )(i1, i2, c, v, v, x, nw.reshape(1, D))


def peer_ffn(x, nw, wq_t, keys, u, v, final_nw=None):
    M, D = x.shape
    Mp = -(-M // LANES) * LANES
    xp = jnp.pad(x, ((0, Mp - M), (0, 0))) if Mp != M else x
    xn, i1, i2, g = peer_route(xp, nw, wq_t, keys)
    c = peer_hidden(xn, u, i1, i2, g)
    out = peer_out(i1, i2, c, v, xp, final_nw)
    return out[:M] if Mp != M else out


def _pad_time(a, B, T, Tp):
    if Tp == T:
        return a
    W = a.shape[1]
    return jnp.pad(a.reshape(B, T, W), ((0, 0), (0, Tp - T), (0, 0))).reshape(B * Tp, W)


def _unpad_time(a, B, T, Tp):
    if Tp == T:
        return a
    W = a.shape[1]
    return a.reshape(B, Tp, W)[:, :T].reshape(B * T, W)


def _trunk(x, past, hgrn0, ssm0, conv0, wts):
    B, T, D = x.shape
    M = B * T
    H = FOX_HEADS
    if T % SEQ_CHUNK == 0:
        Tp = T
    else:
        assert T <= SEQ_CHUNK
        Tp = SEQ_CHUNK
    x2 = x.reshape(M, D)

    proj = norm_matmul(x2, wts["norm_mix"][0], wts["w_in_a"], "in_proj_a")
    ff = proj[:, 7 * FOX_W:7 * FOX_W + H]
    if past is None:
        ff_t = ff.reshape(B, T, H).transpose(0, 2, 1).reshape(B * H, T)
        lf_t, c_t = fox_gate(ff_t, jnp.tile(wts["b_fox_f"], B).reshape(B * H, 1))
        flogf = lf_t.reshape(B, H, T).transpose(0, 2, 1)
        c_bth = c_t.reshape(B, H, T).transpose(0, 2, 1)
        fo, fk, fv = fox_prompt_attn(proj, c_bth, c_t.reshape(B * H, 1, T), B, T)
        fk = fk.reshape(B, T, H, FOX_HD)
        fv = fv.reshape(B, T, H, FOX_HD)
    else:
        fk = proj[:, FOX_W:2 * FOX_W].reshape(B, T, H, FOX_HD)
        fv = proj[:, 2 * FOX_W:3 * FOX_W].reshape(B, T, H, FOX_HD)
        assert T * H <= LANES
        k_pool, v_pool, lf_pool, page_table = past
        n_phys = k_pool.shape[0]
        pad = LANES - T * H
        ff_l = jnp.pad(ff.reshape(B, T * H), ((0, 0), (0, pad)))
        lf_l, c_l = fox_gate(ff_l, jnp.tile(wts["b_fox_f"], LANES // H).reshape(1, LANES), stride=H)
        flogf = lf_l[:, :T * H].reshape(B, T, H)
        sfx_pool, tot_pool = page_suffix_sums(lf_pool.reshape(n_phys, PAGE * H), H)
        q_r = proj[:, 0:FOX_W].reshape(B, T * H, FOX_HD).astype(bf16)
        k_new = jnp.pad(proj[:, FOX_W:2 * FOX_W].reshape(B, T * H, FOX_HD), ((0, 0), (0, pad), (0, 0)))
        v_new = jnp.pad(proj[:, 2 * FOX_W:3 * FOX_W].reshape(B, T * H, FOX_HD), ((0, 0), (0, pad), (0, 0)))
        fo = fox_decode_attn(q_r, k_new, v_new, c_l.reshape(B, 1, LANES),
                             k_pool.reshape(n_phys, PAGE * H, FOX_HD), v_pool.reshape(n_phys, PAGE * H, FOX_HD),
                             sfx_pool.reshape(n_phys, 1, PAGE * H), tot_pool.reshape(n_phys, 1, PAGE * H),
                             page_table).reshape(M, FOX_W)

    if Tp == T:
        hsrc, hcb0 = proj, 3 * H
    else:
        hsrc, hcb0 = _pad_time(proj[:, 3 * FOX_W:3 * FOX_W + 4 * HGRN_W], B, T, Tp), 0
    ho, h_state = gla(hsrc, hcb0, wts["hgrn_lb_logits"], wts["hgrn_norm_w"], hgrn0, B, Tp, T, 0)
    ho = _unpad_time(ho, B, T, Tp)
    x2 = matmul_residual([fo, ho], wts["w_out_a"], x2, "out_proj_a")
    x2 = peer_ffn(x2, wts["norm_ffn"][0], wts["peer_wq_t"][0], wts["peer_keys"][0], wts["peer_u"][0],
                  wts["peer_v"][0])

    projc = norm_matmul(x2, wts["norm_mix"][1], wts["w_in_c"], "in_proj_c")
    projc3 = projc.reshape(B, T, projc.shape[1])
    if T >= SSD_CONV - 1:
        conv_state = projc3[:, T - (SSD_CONV - 1):, SSD_INNER:SSD_INNER + SSD_CONV_DIM]
    else:
        conv_state = jnp.concatenate([conv0, projc3[:, :, SSD_INNER:SSD_INNER + SSD_CONV_DIM]],
                                     axis=1)[:, -(SSD_CONV - 1):]
    csrc = _pad_time(projc, B, T, Tp)
    conv0_p = jnp.pad(conv0, ((0, 0), (8 - (SSD_CONV - 1), 0), (0, 0)))
    yg, s_state = ssd(csrc, wts["conv_w"], wts["conv_b"], conv0_p, wts["dt_bias8"], wts["a_log8"],
                      wts["d_skip8"], wts["ssd_norm_w"], ssm0.reshape(B, SSD_GROUPS, SSD_GW, SSD_N), B, Tp, T)
    x2 = matmul_residual([_unpad_time(yg, B, T, Tp)], wts["w_out_c"], x2, "out_proj_c")
    y = peer_ffn(x2, wts["norm_ffn"][1], wts["peer_wq_t"][1], wts["peer_keys"][1], wts["peer_u"][1],
                 wts["peer_v"][1], final_nw=wts["norm_final"]).reshape(B, T, D)
    return (y, fk[None], fv[None], flogf[None], h_state[None],
            s_state.reshape(B, SSD_HEADS, SSD_HD, SSD_N)[None], conv_state[None])


def _row8(a):
    return jnp.tile(jnp.pad(a.astype(f32), (0, LANES - a.shape[0]))[None, :], (8, 1))


def kernel(x_prompt, x_sample, cache_fox_k, cache_fox_v, cache_fox_logf, state_hgrn, state_ssm, state_conv, page_table, norm_mix, norm_ffn, norm_final, w_in_a, b_fox_f, hgrn_lb_logits, hgrn_norm_w, w_out_a, w_in_c, conv_w, conv_b, dt_bias, a_log, d_skip, ssd_norm_w, w_out_c, peer_wq, peer_keys, peer_u, peer_v):
    D = x_prompt.shape[-1]
    wa = w_in_a[0]
    n_main = 3 * FOX_W
    wa_r = jnp.concatenate(
        [wa[:, :n_main], wa[:, n_main + FOX_HEADS:],
         jnp.pad(wa[:, n_main:n_main + FOX_HEADS], ((0, 0), (0, N_TAIL - FOX_HEADS)))], axis=1).astype(bf16)
    wc = w_in_c[0]
    wc_r = jnp.pad(wc, ((0, 0), (0, N_TAIL - SSD_HEADS))).astype(bf16)
    wts = dict(
        norm_mix=norm_mix, norm_ffn=norm_ffn, norm_final=norm_final,
        w_in_a=wa_r, b_fox_f=b_fox_f[0], hgrn_lb_logits=hgrn_lb_logits, hgrn_norm_w=hgrn_norm_w[0],
        w_out_a=w_out_a[0].astype(bf16),
        w_in_c=wc_r, conv_w=conv_w[0], conv_b=conv_b[0],
        dt_bias8=_row8(dt_bias[0]), a_log8=_row8(a_log[0]), d_skip8=_row8(d_skip[0]),
        ssd_norm_w=ssd_norm_w[0], w_out_c=w_out_c[0].astype(bf16),
        peer_wq_t=jnp.swapaxes(peer_wq, 1, 2).astype(bf16), peer_keys=peer_keys.astype(bf16),
        peer_u=peer_u.astype(bf16), peer_v=peer_v.astype(bf16),
    )
    Bp = x_prompt.shape[0]
    hgrn0_p = jnp.zeros((Bp, HGRN_HEADS, HGRN_DK, HGRN_DV), f32)
    ssm0_p = jnp.zeros((Bp, SSD_HEADS, SSD_HD, SSD_N), f32)
    conv0_p = jnp.zeros((Bp, SSD_CONV - 1, SSD_CONV_DIM), f32)
    y_p, fk_p, fv_p, fl_p, h_p, s_p, c_p = _trunk(x_prompt, None, hgrn0_p, ssm0_p, conv0_p, wts)
    past = (cache_fox_k[0], cache_fox_v[0], cache_fox_logf[0], page_table)
    y_s, fk_s, fv_s, fl_s, h_s, s_s, c_s = _trunk(x_sample, past, state_hgrn[0], state_ssm[0], state_conv[0], wts)
    return (y_p, y_s, fk_p, fv_p, fl_p, fk_s, fv_s, fl_s, h_p, h_s, s_p, s_s, c_p, c_s)
```

```python
import functools
import math

import jax
import jax.numpy as jnp
from jax import lax
from jax.experimental import pallas as pl
from jax.experimental.pallas import tpu as pltpu

f32 = jnp.float32
bf16 = jnp.bfloat16
i32 = jnp.int32

EPS = 1e-6
LANES = 128
VMEM_LIMIT = 48 * 1024 * 1024

FOX_HD = 128
FOX_HEADS = 8
FOX_W = FOX_HEADS * FOX_HD
HGRN_HEADS = 8
HGRN_DK = 128
HGRN_DV = 128
HGRN_W = HGRN_HEADS * HGRN_DK
SSD_HD = 64
SSD_HEADS = 64
SSD_GROUPS = 8
SSD_R = SSD_HEADS // SSD_GROUPS
SSD_N = 128
SSD_INNER = SSD_HEADS * SSD_HD
SSD_GW = SSD_INNER // SSD_GROUPS
SSD_CONV = 4
SSD_CONV_DIM = SSD_INNER + 2 * SSD_GROUPS * SSD_N
PEER_HEADS = 8
PEER_NKEYS = 128
PEER_DQ = 256
PEER_TOPK = 16
PEER_SLOTS = PEER_HEADS * PEER_TOPK
PAGE = 128
SEQ_CHUNK = 128
DIAG = 16
GLA_SUB = 64
GLA_SAFE_LOG = 80.0
N_TAIL = 512

_NT = (((1,), (1,)), ((), ()))
_TN = (((0,), (0,)), ((), ()))


def _params(*sem):
    return pltpu.CompilerParams(dimension_semantics=sem, vmem_limit_bytes=VMEM_LIMIT)


def _tile(n, target):
    t = min(n, target)
    while n % t:
        t -= 8
    return t


def _cumsum_axis(x, axis, stride=1):
    n = x.shape[axis]
    idx = lax.broadcasted_iota(i32, x.shape, axis)
    s = stride
    while s < n:
        x = x + jnp.where(idx >= s, pltpu.roll(x, s, axis=axis), 0.0)
        s *= 2
    return x


def _row_reduce(x, combine, reduce):
    acc = x[:, 0:LANES]
    for i in range(1, x.shape[1] // LANES):
        acc = combine(acc, x[:, i * LANES:(i + 1) * LANES])
    return reduce(acc, axis=1, keepdims=True)


def _rms(x, w):
    ms = jnp.mean(x * x, axis=-1, keepdims=True)
    return x * lax.rsqrt(ms + EPS) * w


def _norm_matmul_body(x_ref, nw_ref, w_ref, o_ref, xn_ref):
    @pl.when(pl.program_id(1) == 0)
    def _():
        xn_ref[...] = _rms(x_ref[...], nw_ref[...]).astype(bf16)

    o_ref[...] = jnp.dot(xn_ref[...], w_ref[...], preferred_element_type=f32)


def norm_matmul(x, nw, w, name):
    M, K = x.shape
    N = w.shape[1]
    tm, tn = _tile(M, 1024), _tile(N, 768)
    return pl.pallas_call(
        _norm_matmul_body,
        grid=(M // tm, N // tn),
        in_specs=[pl.BlockSpec((tm, K), lambda i, j: (i, 0)),
                  pl.BlockSpec((1, K), lambda i, j: (0, 0)),
                  pl.BlockSpec((K, tn), lambda i, j: (0, j))],
        out_specs=pl.BlockSpec((tm, tn), lambda i, j: (i, j)),
        out_shape=jax.ShapeDtypeStruct((M, N), f32),
        scratch_shapes=[pltpu.VMEM((tm, K), bf16)],
        compiler_params=_params("parallel", "arbitrary"),
        name=name,
    )(x, nw.reshape(1, K), w)


def _matmul_res_body(*refs, n_parts):
    a_refs, w_ref, r_ref, o_ref = refs[:n_parts], refs[n_parts], refs[n_parts + 1], refs[n_parts + 2]
    acc = r_ref[...]
    k0 = 0
    for a_ref in a_refs:
        k1 = k0 + a_ref.shape[1]
        acc = acc + jnp.dot(a_ref[...].astype(bf16), w_ref[k0:k1, :], preferred_element_type=f32)
        k0 = k1
    o_ref[...] = acc


def matmul_residual(parts, w, res, name):
    M = parts[0].shape[0]
    K, N = w.shape
    tm, tn = _tile(M, 1024), _tile(N, 512)
    return pl.pallas_call(
        functools.partial(_matmul_res_body, n_parts=len(parts)),
        grid=(M // tm, N // tn),
        in_specs=([pl.BlockSpec((tm, a.shape[1]), lambda i, j: (i, 0)) for a in parts]
                  + [pl.BlockSpec((K, tn), lambda i, j: (0, j)),
                     pl.BlockSpec((tm, tn), lambda i, j: (i, j))]),
        out_specs=pl.BlockSpec((tm, tn), lambda i, j: (i, j)),
        out_shape=jax.ShapeDtypeStruct((M, N), f32),
        compiler_params=_params("parallel", "arbitrary"),
        name=name,
    )(*parts, w, res)


def _fox_gate_body(ff_ref, b_ref, lf_ref, c_ref, *, stride):
    lf = jax.nn.log_sigmoid(ff_ref[...] + b_ref[...])
    lf_ref[...] = lf
    c_ref[...] = _cumsum_axis(lf, 1, stride)


def fox_gate(ff, bias, stride=1):
    R, T = ff.shape
    shp = jax.ShapeDtypeStruct((R, T), f32)
    return pl.pallas_call(functools.partial(_fox_gate_body, stride=stride), out_shape=[shp, shp], name="fox_gate",
                          compiler_params=pltpu.CompilerParams(vmem_limit_bytes=VMEM_LIMIT))(ff, bias)


def _fox_attn_body(q_ref, k_ref, v_ref, ct_ref, cs_ref, o_ref, ko_ref, vo_ref, vt_sc, cs_sc, m_sc, l_sc, acc_sc,
                   *, tq, scale, hpb):
    hg = pl.program_id(1)
    qi = pl.program_id(2)
    D = FOX_HD
    T = k_ref.shape[0]

    @pl.when(qi == 0)
    def _():
        ko_ref[...] = k_ref[...]
        vo_ref[...] = v_ref[...]
        cs = cs_ref[0]
        lane = lax.broadcasted_iota(i32, cs.shape, 1)
        for j in range(hpb):
            col = jnp.sum(jnp.where(lane == hg * hpb + j, cs, 0.0), axis=1, keepdims=True)
            cs_sc[j] = jnp.broadcast_to(col, (T, LANES))
            for t in range(T // LANES):
                blk = v_ref[t * LANES:(t + 1) * LANES, j * D:(j + 1) * D]
                vt_sc[j, :, t * LANES:(t + 1) * LANES] = blk.T.astype(bf16)

    q0 = pl.multiple_of(qi * tq, tq)
    qs, cts = [], []
    for j in range(hpb):
        qs.append(q_ref[:, j * D:(j + 1) * D].astype(bf16))
        cts.append(ct_ref[j, :, pl.ds(q0, tq)])
        m_sc[j] = jnp.full((1, tq), -jnp.inf, f32)
        l_sc[j] = jnp.zeros((1, tq), f32)
        acc_sc[j] = jnp.zeros((D, tq), f32)
    causal = lax.broadcasted_iota(i32, (tq, tq), 0) <= lax.broadcasted_iota(i32, (tq, tq), 1)

    def block(ki, masked):
        start = pl.multiple_of(ki * tq, tq)
        for j in range(hpb):
            k = k_ref[pl.ds(start, tq), j * D:(j + 1) * D].astype(bf16)
            st = lax.dot_general(k, qs[j], _NT, preferred_element_type=f32) * scale
            csb = cs_sc[j, pl.ds(start, tq), :]
            st = st + cts[j] - jnp.concatenate([csb] * (tq // LANES), axis=1)
            if masked:
                st = jnp.where(causal, st, -jnp.inf)
            m_prev = m_sc[j]
            m_new = jnp.maximum(m_prev, jnp.max(st, axis=0, keepdims=True))
            alpha = jnp.exp(m_prev - m_new)
            p = jnp.exp(st - m_new)
            l_sc[j] = alpha * l_sc[j] + jnp.sum(p, axis=0, keepdims=True)
            acc_sc[j] = alpha * acc_sc[j] + jnp.dot(vt_sc[j, :, pl.ds(start, tq)], p.astype(bf16),
                                                    preferred_element_type=f32)
            m_sc[j] = m_new

    def body(ki, carry):
        block(ki, False)
        return carry

    lax.fori_loop(0, qi, body, 0)
    block(qi, True)
    for j in range(hpb):
        o_t = acc_sc[j] / l_sc[j]
        for t in range(tq // LANES):
            o_ref[t * LANES:(t + 1) * LANES, j * D:(j + 1) * D] = o_t[:, t * LANES:(t + 1) * LANES].T


def fox_prompt_attn(proj, c_bth, c_rows, B, T):
    tq = _tile(T, 256)
    nq = T // tq
    H = FOX_HEADS
    hpb = 4
    W = hpb * FOX_HD
    G = H // hpb
    body = functools.partial(_fox_attn_body, tq=tq, scale=FOX_HD ** -0.5, hpb=hpb)
    return pl.pallas_call(
        body,
        grid=(B, G, nq),
        in_specs=[pl.BlockSpec((tq, W), lambda b, g, qi: (b * nq + qi, g)),
                  pl.BlockSpec((T, W), lambda b, g, qi: (b, G + g)),
                  pl.BlockSpec((T, W), lambda b, g, qi: (b, 2 * G + g)),
                  pl.BlockSpec((hpb, 1, T), lambda b, g, qi: (b * G + g, 0, 0)),
                  pl.BlockSpec((1, T, H), lambda b, g, qi: (b, 0, 0))],
        out_specs=[pl.BlockSpec((tq, W), lambda b, g, qi: (b * nq + qi, g)),
                   pl.BlockSpec((T, W), lambda b, g, qi: (b, g)),
                   pl.BlockSpec((T, W), lambda b, g, qi: (b, g))],
        out_shape=[jax.ShapeDtypeStruct((B * T, FOX_W), f32)] * 3,
        scratch_shapes=[pltpu.VMEM((hpb, FOX_HD, T), bf16), pltpu.VMEM((hpb, T, LANES), f32),
                        pltpu.VMEM((hpb, 1, tq), f32), pltpu.VMEM((hpb, 1, tq), f32),
                        pltpu.VMEM((hpb, FOX_HD, tq), f32)],
        compiler_params=_params("parallel", "parallel", "arbitrary"),
        name="fox_prompt_attn",
    )(proj, proj, proj, c_rows, c_bth)


def _suffix_body(x_ref, sfx_ref, tot_ref, *, stride):
    x = x_ref[...]
    lane = lax.broadcasted_iota(i32, x.shape, 1)
    n = x.shape[1]
    y = jnp.where(lane < n - stride, pltpu.roll(x, n - stride, axis=1), 0.0)
    t = x
    s = stride
    while s < n:
        y = y + jnp.where(lane < n - s, pltpu.roll(y, n - s, axis=1), 0.0)
        t = t + pltpu.roll(t, s, axis=1)
        s *= 2
    sfx_ref[...] = y
    tot_ref[...] = t


def page_suffix_sums(lf_rows, stride):
    R, n = lf_rows.shape
    tr = _tile(R, 512)
    shp = jax.ShapeDtypeStruct((R, n), f32)
    return pl.pallas_call(
        functools.partial(_suffix_body, stride=stride),
        grid=(R // tr,),
        in_specs=[pl.BlockSpec((tr, n), lambda i: (i, 0))],
        out_specs=[pl.BlockSpec((tr, n), lambda i: (i, 0)), pl.BlockSpec((tr, n), lambda i: (i, 0))],
        out_shape=[shp, shp],
        compiler_params=_params("parallel"),
        name="fox_page_suffix",
    )(lf_rows)


def _fox_dec_body(pt_ref, q_ref, kn_ref, vn_ref, cn_ref, *refs, scale, G):
    del pt_ref
    kp_refs, vp_refs = refs[0:G], refs[G:2 * G]
    sfx_refs, tot_refs = refs[2 * G:3 * G], refs[3 * G:4 * G]
    o_ref, m_sc, l_sc, acc_sc, carry_sc, cc_sc = refs[4 * G:]
    H = FOX_HEADS
    j = pl.program_id(1)
    q = q_ref[0]

    def partial_softmax(k, v, bias):
        s = lax.dot_general(q, k.astype(bf16), _NT, preferred_element_type=f32) * scale + bias
        m = _row_reduce(s, jnp.maximum, jnp.max)
        p = jnp.exp(s - m)
        return m, _row_reduce(p, jnp.add, jnp.sum), jnp.dot(p.astype(bf16), v.astype(bf16),
                                                            preferred_element_type=f32)

    def merge(parts):
        m_prev = m_sc[...]
        m_new = m_prev
        for m, _, _ in parts:
            m_new = jnp.maximum(m_new, m)
        alpha = jnp.exp(m_prev - m_new)
        l = alpha * l_sc[...]
        acc = alpha * acc_sc[...]
        for m, lp, ap in parts:
            w = jnp.exp(m - m_new)
            l = l + w * lp
            acc = acc + w * ap
        m_sc[...] = m_new
        l_sc[...] = l
        acc_sc[...] = acc

    def step(k, v, bias):
        merge([partial_softmax(k, v, bias)])

    @pl.when(j == 0)
    def _():
        m_sc[...] = jnp.full(m_sc.shape, -jnp.inf, f32)
        l_sc[...] = jnp.zeros(l_sc.shape, f32)
        acc_sc[...] = jnp.zeros(acc_sc.shape, f32)
        carry_sc[...] = jnp.zeros(carry_sc.shape, f32)
        cn = cn_ref[0]
        shape = (q.shape[0], cn.shape[1])
        row = lax.broadcasted_iota(i32, shape, 0)
        col = lax.broadcasted_iota(i32, shape, 1)
        cc = jnp.sum(jnp.where(row == col, cn, 0.0), axis=1, keepdims=True)
        cc_sc[...] = cc
        ok = jnp.logical_and(col % H == row % H, col // H <= row // H)
        step(kn_ref[0], vn_ref[0], jnp.where(ok, cc - cn, -jnp.inf))

    @pl.when(j > 0)
    def _():
        carry = carry_sc[...]
        shape = (q.shape[0], carry.shape[1])
        same_head = lax.broadcasted_iota(i32, shape, 1) % H == lax.broadcasted_iota(i32, shape, 0) % H
        cc = cc_sc[...]
        parts = []
        for g in range(G):
            bias = jnp.where(same_head, cc + (carry + sfx_refs[g][0]), -jnp.inf)
            parts.append(partial_softmax(kp_refs[g][0], vp_refs[g][0], bias))
            carry = carry + tot_refs[g][0]
        merge(parts)
        carry_sc[...] = carry

    @pl.when(j == pl.num_programs(1) - 1)
    def _():
        o_ref[0] = acc_sc[...] / l_sc[...]


def fox_decode_attn(q, k_new, v_new, c_new, k_pool, v_pool, sfx_pool, tot_pool, page_table):
    Bd, n_pages = page_table.shape
    R, D = q.shape[1], q.shape[2]
    PW = k_pool.shape[1]
    G = math.gcd(n_pages, 8)

    def page(g):
        return lambda b, j, pt: (pt[b, n_pages - 1 - (jnp.maximum(j, 1) - 1) * G - g], 0, 0)

    same = lambda b, j, pt: (b, 0, 0)
    grid_spec = pltpu.PrefetchScalarGridSpec(
        num_scalar_prefetch=1,
        grid=(Bd, n_pages // G + 1),
        in_specs=([pl.BlockSpec((1, R, D), same),
                   pl.BlockSpec((1, LANES, D), same),
                   pl.BlockSpec((1, LANES, D), same),
                   pl.BlockSpec((1, 1, LANES), same)]
                  + [pl.BlockSpec((1, PW, D), page(g)) for g in range(G)]
                  + [pl.BlockSpec((1, PW, D), page(g)) for g in range(G)]
                  + [pl.BlockSpec((1, 1, PW), page(g)) for g in range(G)]
                  + [pl.BlockSpec((1, 1, PW), page(g)) for g in range(G)]),
        out_specs=pl.BlockSpec((1, R, D), same),
        scratch_shapes=[pltpu.VMEM((R, 1), f32), pltpu.VMEM((R, 1), f32), pltpu.VMEM((R, D), f32),
                        pltpu.VMEM((1, PW), f32), pltpu.VMEM((R, 1), f32)],
    )
    body = functools.partial(_fox_dec_body, scale=FOX_HD ** -0.5, G=G)
    return pl.pallas_call(
        body,
        grid_spec=grid_spec,
        out_shape=jax.ShapeDtypeStruct((Bd, R, D), f32),
        compiler_params=_params("parallel", "arbitrary"),
        name="fox_decode_attn",
    )(page_table, q, k_new, v_new, c_new, *([k_pool] * G), *([v_pool] * G), *([sfx_pool] * G),
      *([tot_pool] * G))


def _gla_body(q_ref, f_ref, i_ref, g_ref, lbl_ref, nw_ref, s0_ref, o_ref, so_ref,
              s_sc, q_sc, k_sc, b_sc, v_sc, oa_sc, *, C, t_valid, layer_j, HB):
    c = pl.program_id(2)
    SUB = GLA_SUB
    K = HGRN_DK

    @pl.when(c == 0)
    def _():
        for j in range(HB):
            s_sc[j] = s0_ref[0, j].T

    lg = lbl_ref[...]
    e = jnp.exp(lg - jnp.max(lg, axis=0, keepdims=True))
    sm = e / jnp.sum(e, axis=0, keepdims=True)
    lb = jnp.sum(sm[0:layer_j + 1], axis=0, keepdims=True)

    z = f_ref[...]
    hlogf = jnp.log(lb + (1.0 - lb) * jax.nn.sigmoid(z))
    hk = (1.0 - lb) * jax.nn.sigmoid(-z)
    valid = (c * C + lax.broadcasted_iota(i32, (C, 1), 0)) < t_valid
    lc = jnp.where(valid, hlogf, 0.0)
    kk = jnp.where(valid, hk, 0.0)
    subs = list(range(0, C, SUB))
    bs = [_cumsum_axis(lc[lo:lo + SUB, :], 0) for lo in subs]
    bl_min = bs[0][SUB - 1:SUB, :]
    for b_ in bs[1:]:
        bl_min = jnp.minimum(bl_min, b_[SUB - 1:SUB, :])
    in_range = jnp.min(bl_min) >= -GLA_SAFE_LOG
    tril = lax.broadcasted_iota(i32, (SUB, SUB), 1) <= lax.broadcasted_iota(i32, (SUB, SUB), 0)

    @pl.when(in_range)
    def _():
        for j in range(HB):
            cols = slice(j * K, (j + 1) * K)
            st = s_sc[j]
            for si, lo in enumerate(subs):
                rows = slice(lo, lo + SUB)
                b = bs[si][:, cols]
                bl = b[SUB - 1:SUB, :]
                ks = kk[rows, cols]
                vs = i_ref[rows, cols].astype(bf16)
                qd = (q_ref[rows, cols] * jnp.exp(b)).astype(bf16)
                ki = (ks * jnp.exp(-b)).astype(bf16)
                a = jnp.where(tril, lax.dot_general(qd, ki, _NT, preferred_element_type=f32), 0.0)
                o = (lax.dot_general(qd, st.astype(bf16), _NT, preferred_element_type=f32)
                     + jnp.dot(a.astype(bf16), vs, preferred_element_type=f32))
                kd = (ks * jnp.exp(bl - b)).astype(bf16)
                st = st * jnp.exp(bl) + lax.dot_general(vs, kd, _TN, preferred_element_type=f32)
                o_ref[rows, cols] = _rms(o, nw_ref[...]) * jax.nn.silu(g_ref[rows, cols])
            s_sc[j] = st

    def off_diag(lo, hi):
        if hi - lo <= DIAG:
            return
        mid = (lo + hi) // 2
        r = b_sc[pl.ds(mid - 1, 1), :]
        qh = q_sc[mid:hi, :] * jnp.exp(b_sc[mid:hi, :] - r)
        kh = k_sc[lo:mid, :] * jnp.exp(r - b_sc[lo:mid, :])
        a = lax.dot_general(qh.astype(bf16), kh.astype(bf16), _NT, preferred_element_type=f32)
        oa_sc[mid:hi, :] += jnp.dot(a.astype(bf16), v_sc[lo:mid, :].astype(bf16), preferred_element_type=f32)
        off_diag(lo, mid)
        off_diag(mid, hi)

    srow = lax.broadcasted_iota(i32, (DIAG, 1), 0)

    def diag(i, carry):
        lo = pl.multiple_of(i * DIAG, DIAG)
        qb = q_sc[pl.ds(lo, DIAG), :]
        kb = k_sc[pl.ds(lo, DIAG), :]
        bb = b_sc[pl.ds(lo, DIAG), :]
        vb = v_sc[pl.ds(lo, DIAG), :]
        ob = jnp.zeros((DIAG, vb.shape[1]), f32)
        for t in range(DIAG):
            dec = jnp.exp(jnp.where(srow <= t, bb[t:t + 1, :] - bb, -jnp.inf))
            w = jnp.sum(qb[t:t + 1, :] * kb * dec, axis=1, keepdims=True)
            ot = jnp.sum(w * vb, axis=0, keepdims=True)
            ob = jnp.where(srow == t, ot, ob)
        oa_sc[pl.ds(lo, DIAG), :] += ob
        return carry

    @pl.when(jnp.logical_not(in_range))
    def _():
        for j in range(HB):
            cols = slice(j * K, (j + 1) * K)
            q_sc[...] = q_ref[:, cols]
            k_sc[...] = kk[:, cols]
            v_sc[...] = i_ref[:, cols]
            for si, lo in enumerate(subs):
                rows = slice(lo, lo + SUB)
                b = bs[si][:, cols]
                b_sc[rows, :] = b
                bl = b[SUB - 1:SUB, :]
                st = s_sc[j]
                qd = (q_sc[rows, :] * jnp.exp(b)).astype(bf16)
                oa_sc[rows, :] = lax.dot_general(qd, st.astype(bf16), _NT, preferred_element_type=f32)
                off_diag(lo, lo + SUB)
                lax.fori_loop(lo // DIAG, (lo + SUB) // DIAG, diag, 0)
                kd = (k_sc[rows, :] * jnp.exp(bl - b)).astype(bf16)
                s_sc[j] = st * jnp.exp(bl) + lax.dot_general(v_sc[rows, :].astype(bf16), kd, _TN,
                                                             preferred_element_type=f32)
            o_ref[:, cols] = _rms(oa_sc[...], nw_ref[...]) * jax.nn.silu(g_ref[:, cols])

    @pl.when(c == pl.num_programs(2) - 1)
    def _():
        for j in range(HB):
            so_ref[0, j] = s_sc[j].T


def gla(src, cb0, lb_logits, norm_w, s0, B, Tp, t_valid, layer_j):
    H, C = HGRN_HEADS, SEQ_CHUNK
    HB = 4
    G = H // HB
    W = HB * HGRN_DK
    nc = Tp // C
    L = lb_logits.shape[0]
    assert cb0 % HB == 0

    def col(g):
        return pl.BlockSpec((C, W), lambda b, h, c: (b * nc + c, (cb0 + g * H) // HB + h))

    body = functools.partial(_gla_body, C=C, t_valid=t_valid, layer_j=layer_j, HB=HB)
    return pl.pallas_call(
        body,
        grid=(B, G, nc),
        in_specs=[col(0), col(1), col(2), col(3),
                  pl.BlockSpec((L, W), lambda b, h, c: (0, h)),
                  pl.BlockSpec((1, HGRN_DV), lambda b, h, c: (0, 0)),
                  pl.BlockSpec((1, HB, HGRN_DK, HGRN_DV), lambda b, h, c: (b, h, 0, 0))],
        out_specs=[pl.BlockSpec((C, W), lambda b, h, c: (b * nc + c, h)),
                   pl.BlockSpec((1, HB, HGRN_DK, HGRN_DV), lambda b, h, c: (b, h, 0, 0))],
        out_shape=[jax.ShapeDtypeStruct((B * Tp, HGRN_W), f32),
                   jax.ShapeDtypeStruct((B, H, HGRN_DK, HGRN_DV), f32)],
        scratch_shapes=[pltpu.VMEM((HB, HGRN_DV, HGRN_DK), f32)] + [pltpu.VMEM((C, HGRN_DK), f32)] * 5,
        compiler_params=_params("parallel", "parallel", "arbitrary"),
        name="hgrn2_gla",
    )(src, src, src, src, lb_logits, norm_w.reshape(1, HGRN_DV), s0)


def _causal_conv_silu(x_ref, w_ref, b_ref, c0_ref, xp_sc, first, C, rows):
    @pl.when(first)
    def _():
        xp_sc[0:8, :] = c0_ref[0]

    xp_sc[8:8 + rows, :] = x_ref[0:rows, :]
    w = w_ref[...]
    acc = b_ref[...] + w[0:1, :] * xp_sc[pl.ds(8 - (SSD_CONV - 1), rows), :]
    for i in range(1, SSD_CONV):
        acc = acc + w[i:i + 1, :] * xp_sc[pl.ds(8 - (SSD_CONV - 1) + i, rows), :]
    if rows == C:
        xp_sc[0:8, :] = xp_sc[C:C + 8, :]
    return jax.nn.silu(acc)


def _ssd_body(z_ref, x_ref, bm_ref, cm_ref, dt_ref, wx_ref, wb_ref, wc_ref, bx_ref, bb_ref, bc_ref,
              c0x_ref, c0b_ref, c0c_ref, dtb_ref, alog_ref, dsk_ref, nw_ref, s0_ref,
              y_ref, so_ref, st_sc, xpx_sc, xpb_sc, xpc_sc, *, C, t_valid, rows):
    g = pl.program_id(1)
    c = pl.program_id(2)
    R = SSD_R
    Ru = rows
    npair = SSD_GW // LANES

    @pl.when(c == 0)
    def _():
        s0 = s0_ref[0, 0]
        for i in range(npair):
            st_sc[:, i * LANES:(i + 1) * LANES] = s0[i * LANES:(i + 1) * LANES, :].T

    xs = _causal_conv_silu(x_ref, wx_ref, bx_ref, c0x_ref, xpx_sc, c == 0, C, Ru)
    bm = _causal_conv_silu(bm_ref, wb_ref, bb_ref, c0b_ref, xpb_sc, c == 0, C, Ru).astype(bf16)
    cm = _causal_conv_silu(cm_ref, wc_ref, bc_ref, c0c_ref, xpc_sc, c == 0, C, Ru).astype(bf16)

    shift = (LANES - g * R) % LANES
    dt_all = jax.nn.softplus(dt_ref[...] + dtb_ref[0:1, :])
    valid = (c * C + lax.broadcasted_iota(i32, (C, 1), 0)) < t_valid
    dt_all = jnp.where(valid, dt_all, 0.0)
    la = dt_all * (-jnp.exp(alog_ref[0:1, :]))
    cs_all = _cumsum_axis(la, 0)
    cs_full = pltpu.roll(cs_all, shift, axis=1)
    dt_full = pltpu.roll(dt_all, shift, axis=1)
    dsk8 = pltpu.roll(dsk_ref[...], shift, axis=1)
    cs_t = cs_full.T[:, 0:Ru]
    dt_t = dt_full.T[:, 0:Ru]
    cs = cs_full[0:Ru, :]
    dt = dt_full[0:Ru, :]
    cl = cs_full[C - 1:C, :]
    ecs = jnp.exp(cs)
    wst = dt * jnp.exp(cl - cs)
    ecl8 = jnp.exp(cs_full[C - 8:C, :])

    cb = lax.dot_general(cm, bm, _NT, preferred_element_type=f32)
    causal = lax.broadcasted_iota(i32, (Ru, Ru), 0) >= lax.broadcasted_iota(i32, (Ru, Ru), 1)
    low = lax.broadcasted_iota(i32, (Ru, LANES), 1) < SSD_HD

    def expand(a):
        lw = low[0:a.shape[0], :]
        return jnp.concatenate(
            [jnp.where(lw, a[:, 2 * p:2 * p + 1], a[:, 2 * p + 1:2 * p + 2]) for p in range(npair)], axis=1)

    ys = []
    for p in range(npair):
        xp = xs[:, p * LANES:(p + 1) * LANES]
        acc = None
        for hh in range(2):
            r = 2 * p + hh
            dec = jnp.exp(jnp.where(causal, cs[:, r:r + 1] - cs_t[r:r + 1, :], -jnp.inf))
            wm = (cb * dec * dt_t[r:r + 1, :]).astype(bf16)
            xm = jnp.where(low if hh == 0 else jnp.logical_not(low), xp, 0.0).astype(bf16)
            part = jnp.dot(wm, xm, preferred_element_type=f32)
            acc = part if acc is None else acc + part
        ys.append(acc)
    y = jnp.concatenate(ys, axis=1)

    st = st_sc[...]
    y = y + jnp.dot(cm, st.astype(bf16), preferred_element_type=f32) * expand(ecs)
    y = y + expand(dsk8)[0:1, :] * xs

    xw = (xs * expand(wst)).astype(bf16)
    st_new = st * expand(ecl8)[7:8, :] + lax.dot_general(bm, xw, _TN, preferred_element_type=f32)
    st_sc[...] = st_new

    yg = y * jax.nn.silu(z_ref[0:Ru, :])
    y_ref[0:Ru, :] = _rms(yg, nw_ref[...]).astype(y_ref.dtype)
    if Ru < C:
        y_ref[Ru:C, :] = jnp.zeros((C - Ru, y_ref.shape[1]), y_ref.dtype)

    @pl.when(c == pl.num_programs(2) - 1)
    def _():
        for i in range(npair):
            so_ref[0, 0, i * LANES:(i + 1) * LANES, :] = st_new[:, i * LANES:(i + 1) * LANES].T


def ssd(src, conv_w, conv_b, conv0_p, dt_bias8, a_log8, d_skip8, norm_w, s0, B, Tp, t_valid):
    G, C = SSD_GROUPS, SEQ_CHUNK
    nc = Tp // C
    GW, N = SSD_GW, SSD_N
    dt_blk = (SSD_INNER + SSD_CONV_DIM) // LANES
    x0 = SSD_INNER // GW
    b0 = 2 * SSD_INNER // N
    wb0 = SSD_INNER // N
    rows = min(C, -(-t_valid // 16) * 16) if nc == 1 else C
    body = functools.partial(_ssd_body, C=C, t_valid=t_valid, rows=rows)
    par = lambda b, g, c: (0, 0)
    row = lambda off: (lambda b, g, c: (b * nc + c, off + g))
    wcol = lambda off: (lambda b, g, c: (0, off + g))
    c0col = lambda off: (lambda b, g, c: (b, 0, off + g))
    return pl.pallas_call(
        body,
        grid=(B, G, nc),
        in_specs=[pl.BlockSpec((C, GW), row(0)),
                  pl.BlockSpec((C, GW), row(x0)),
                  pl.BlockSpec((C, N), row(b0)),
                  pl.BlockSpec((C, N), row(b0 + G)),
                  pl.BlockSpec((C, LANES), lambda b, g, c: (b * nc + c, dt_blk)),
                  pl.BlockSpec((SSD_CONV, GW), wcol(0)),
                  pl.BlockSpec((SSD_CONV, N), wcol(wb0)),
                  pl.BlockSpec((SSD_CONV, N), wcol(wb0 + G)),
                  pl.BlockSpec((1, GW), wcol(0)),
                  pl.BlockSpec((1, N), wcol(wb0)),
                  pl.BlockSpec((1, N), wcol(wb0 + G)),
                  pl.BlockSpec((1, 8, GW), c0col(0)),
                  pl.BlockSpec((1, 8, N), c0col(wb0)),
                  pl.BlockSpec((1, 8, N), c0col(wb0 + G)),
                  pl.BlockSpec((8, LANES), par), pl.BlockSpec((8, LANES), par), pl.BlockSpec((8, LANES), par),
                  pl.BlockSpec((1, GW), lambda b, g, c: (0, g)),
                  pl.BlockSpec((1, 1, GW, N), lambda b, g, c: (b, g, 0, 0))],
        out_specs=[pl.BlockSpec((C, GW), lambda b, g, c: (b * nc + c, g)),
                   pl.BlockSpec((1, 1, GW, N), lambda b, g, c: (b, g, 0, 0))],
        out_shape=[jax.ShapeDtypeStruct((B * Tp, SSD_INNER), bf16),
                   jax.ShapeDtypeStruct((B, G, GW, N), f32)],
        scratch_shapes=[pltpu.VMEM((N, GW), f32), pltpu.VMEM((C + 8, GW), f32), pltpu.VMEM((C + 8, N), f32),
                        pltpu.VMEM((C + 8, N), f32)],
        compiler_params=_params("parallel", "parallel", "arbitrary"),
        name="ssd_scan",
    )(src, src, src, src, src, conv_w, conv_w, conv_w, conv_b.reshape(1, SSD_CONV_DIM),
      conv_b.reshape(1, SSD_CONV_DIM), conv_b.reshape(1, SSD_CONV_DIM), conv0_p, conv0_p, conv0_p,
      dt_bias8, a_log8, d_skip8, norm_w.reshape(1, SSD_INNER), s0)


def _sort_network(n):
    size = 1
    while size < n:
        size *= 2
    pairs = []
    p = 1
    while p < size:
        k = p
        while k >= 1:
            for j in range(k % p, size - k, 2 * k):
                for i in range(min(k, size - j - k)):
                    if (i + j) // (2 * p) == (i + j + k) // (2 * p):
                        pairs.append((i + j, i + j + k))
            k //= 2
        p *= 2
    return [(a, b) for a, b in pairs if b < n]


def _topk_sorted(s, k):
    n_rows, n = s.shape
    sl = 8
    nv = n_rows // sl
    vals = [s[sl * v:sl * (v + 1), :] for v in range(nv)]
    ids = [jnp.full((sl, n), v, i32) for v in range(nv)]
    for a, b in _sort_network(nv):
        va, vb, ia, ib = vals[a], vals[b], ids[a], ids[b]
        swap = (vb > va) | ((vb == va) & (ib < ia))
        vals[a], vals[b] = jnp.where(swap, vb, va), jnp.where(swap, va, vb)
        ids[a], ids[b] = jnp.where(swap, ib, ia), jnp.where(swap, ia, ib)
    sub = lax.broadcasted_iota(i32, (sl, n), 0)
    kio = lax.broadcasted_iota(i32, (k, n), 0)
    out_v = jnp.zeros((k, n), f32)
    out_i = jnp.zeros((k, n), i32)
    neg = jnp.full((sl, n), -jnp.inf, f32)
    for it in range(k):
        rid = ids[0] * sl + sub
        m = jnp.max(vals[0], axis=0, keepdims=True)
        idx = jnp.min(jnp.where(vals[0] == m, rid, n_rows), axis=0, keepdims=True)
        out_v = jnp.where(kio == it, m, out_v)
        out_i = jnp.where(kio == it, idx, out_i)
        sel = rid == idx
        for p in range(min(nv, k - 1 - it)):
            nxt_v = vals[p + 1] if p + 1 < nv else neg
            nxt_i = ids[p + 1] if p + 1 < nv else ids[p]
            vals[p] = jnp.where(sel, nxt_v, vals[p])
            ids[p] = jnp.where(sel, nxt_i, ids[p])
    return out_v, out_i


_CAND_HALF = PEER_TOPK // 2


def _cand_rows(v0, v1):
    K, Hh = PEER_TOPK, _CAND_HALF
    return jnp.concatenate([v0[0:1, :] + v1] + [v0[p:p + 1, :] + v1[0:Hh, :] for p in range(1, Hh)]
                           + [v0[Hh:K, :] + v1[0:1, :]], axis=0)


def _cand_pq(pos):
    K, Hh = PEER_TOPK, _CAND_HALF
    mid = pos - K
    tail = K + Hh * (Hh - 1)
    p = jnp.where(pos < K, 0, jnp.where(pos < tail, 1 + mid // Hh, Hh + (pos - tail)))
    q = jnp.where(pos < K, pos, jnp.where(pos < tail, mid % Hh, 0))
    return p, q


def _peer_route_body(x_ref, nw_ref, wqt_ref, keys_ref, xn_ref, i1_ref, i2_ref, g_ref,
                     qt_sc, i1t_sc, i2t_sc, gt_sc, *, tm):
    K = PEER_TOPK
    xn = _rms(x_ref[...], nw_ref[...]).astype(bf16)
    xn_ref[...] = xn
    qt_sc[...] = lax.dot_general(wqt_ref[...], xn, _NT, preferred_element_type=f32)

    def head(h, carry):
        base = pl.multiple_of(h * PEER_DQ, PEER_DQ)
        rows = pl.ds(pl.multiple_of(h * K, K), K)
        for half in range(tm // LANES):
            cols = slice(half * LANES, (half + 1) * LANES)
            sts = []
            for part in range(2):
                qp = qt_sc[pl.ds(base + part * (PEER_DQ // 2), PEER_DQ // 2), cols].astype(bf16)
                sts.append(jnp.dot(keys_ref[h, part], qp, preferred_element_type=f32))
            v0, i0 = _topk_sorted(sts[0], K)
            v1, i1 = _topk_sorted(sts[1], K)
            top, pos = _topk_sorted(_cand_rows(v0, v1), K)
            pp, qq = _cand_pq(pos)
            i1s = jnp.zeros((K, LANES), i32)
            i2s = jnp.zeros((K, LANES), i32)
            for r in range(K):
                i1s = jnp.where(pp == r, i0[r:r + 1, :], i1s)
                i2s = jnp.where(qq == r, i1[r:r + 1, :], i2s)
            e = jnp.exp(top - jnp.max(top, axis=0, keepdims=True))
            i1t_sc[rows, cols] = i1s
            i2t_sc[rows, cols] = i2s
            gt_sc[rows, cols] = e / jnp.sum(e, axis=0, keepdims=True)
        return carry

    lax.fori_loop(0, PEER_HEADS, head, 0)
    i1_ref[...] = i1t_sc[...].T
    i2_ref[...] = i2t_sc[...].T
    g_ref[...] = gt_sc[...].T


def peer_route(x, nw, wq_t, keys):
    M, D = x.shape
    tm = 256 if M % 256 == 0 else LANES
    HQ = PEER_HEADS * PEER_DQ
    S = PEER_SLOTS
    body = functools.partial(_peer_route_body, tm=tm)
    return pl.pallas_call(
        body,
        grid=(M // tm,),
        in_specs=[pl.BlockSpec((tm, D), lambda i: (i, 0)),
                  pl.BlockSpec((1, D), lambda i: (0, 0)),
                  pl.BlockSpec((HQ, D), lambda i: (0, 0)),
                  pl.BlockSpec((PEER_HEADS, 2, PEER_NKEYS, PEER_DQ // 2), lambda i: (0, 0, 0, 0))],
        out_specs=[pl.BlockSpec((tm, D), lambda i: (i, 0)),
                   pl.BlockSpec((tm, S), lambda i: (i, 0)),
                   pl.BlockSpec((tm, S), lambda i: (i, 0)),
                   pl.BlockSpec((tm, S), lambda i: (i, 0))],
        out_shape=[jax.ShapeDtypeStruct((M, D), bf16), jax.ShapeDtypeStruct((M, S), i32),
                   jax.ShapeDtypeStruct((M, S), i32), jax.ShapeDtypeStruct((M, S), f32)],
        scratch_shapes=[pltpu.VMEM((HQ, tm), f32), pltpu.VMEM((S, tm), i32), pltpu.VMEM((S, tm), i32),
                        pltpu.VMEM((S, tm), f32)],
        compiler_params=_params("parallel"),
        name="peer_route",
    )(x, nw.reshape(1, D), wq_t, keys)


def _peer_hidden_body(xn_ref, u_ref, i1_ref, i2_ref, g_ref, c_ref, h_sc, s_sc, *, te):
    e = pl.program_id(1)

    @pl.when(e == 0)
    def _():
        h_sc[...] = jnp.zeros(h_sc.shape, f32)
        s_sc[1] = jnp.zeros(s_sc.shape[1:], f32)

    slot = e % 2
    NK = PEER_NKEYS
    nb = te // NK
    i1 = i1_ref[...]
    i2 = i2_ref[...]
    h = h_sc[...]
    for sub in range(nb):
        got = jnp.take_along_axis(s_sc[1 - slot, :, sub * NK:(sub + 1) * NK], i2, axis=1)
        h = jnp.where(i1 == (e - 1) * nb + sub, got, h)
    h_sc[...] = h
    s_sc[slot] = lax.dot_general(xn_ref[...], u_ref[...], _NT, preferred_element_type=f32)

    @pl.when(e == pl.num_programs(1) - 1)
    def _():
        act = 0.5 * h * (1.0 + lax.erf(h * (2.0 ** -0.5)))
        c_ref[...] = g_ref[...] * act


def peer_hidden(xn, u, i1, i2, g):
    M, D = xn.shape
    E = u.shape[0]
    S = PEER_SLOTS
    tm, te = _tile(M, 1024), 1024
    ne = E // te
    tok = lambda i, e: (i, 0)
    body = functools.partial(_peer_hidden_body, te=te)
    return pl.pallas_call(
        body,
        grid=(M // tm, ne + 1),
        in_specs=[pl.BlockSpec((tm, D), tok),
                  pl.BlockSpec((te, D), lambda i, e: (jnp.minimum(e, ne - 1), 0)),
                  pl.BlockSpec((tm, S), tok), pl.BlockSpec((tm, S), tok), pl.BlockSpec((tm, S), tok)],
        out_specs=pl.BlockSpec((tm, S), tok),
        out_shape=jax.ShapeDtypeStruct((M, S), f32),
        scratch_shapes=[pltpu.VMEM((tm, S), f32), pltpu.VMEM((2, tm, te), f32)],
        compiler_params=_params("parallel", "arbitrary"),
        name="peer_hidden",
    )(xn, u, i1, i2, g)


G_HALF = PEER_NKEYS // 2
G_PITCH = G_HALF + 4
G_UNROLL = 32
_HI16 = 0xFFFF0000


def _bf16_bits_high(x):
    return pltpu.bitcast(x.astype(bf16).astype(f32), jnp.uint32)


def _peer_out_body(i1_ref, i2_ref, c_ref, va_ref, vb_ref, x_ref, nw_ref, o_ref, g_sc, *, tm, nr, final_norm):
    e = pl.program_id(1)
    NK = PEER_NKEYS

    @pl.when(e == 0)
    def _():
        o_ref[...] = x_ref[...]
        sub = lax.broadcasted_iota(i32, (NK, PEER_SLOTS), 0)

        def tok_group(gi, carry):
            for u in range(G_UNROLL):
                n = gi * G_UNROLL + u
                r1 = i1_ref[pl.ds(n, 1), :]
                r2 = i2_ref[pl.ds(n, 1), :]
                rc = c_ref[pl.ds(n, 1), :]
                at = jnp.where(sub == r1, rc, 0.0).astype(bf16)
                bt = jnp.where(sub == r2, 1.0, 0.0).astype(bf16)
                tile = lax.dot_general(at, bt, _NT, preferred_element_type=f32)
                word = _bf16_bits_high(tile[0:G_HALF, :]) | (_bf16_bits_high(tile[G_HALF:NK, :]) >> 16)
                g_sc[pl.ds(n * G_PITCH, G_HALF), :] = word
            return carry

        lax.fori_loop(0, tm // G_UNROLL, tok_group, 0)

    his, los = [], []
    for s in range(nr):
        w = g_sc[pl.ds(e * nr + s, tm, stride=G_PITCH), :]
        his.append(pltpu.bitcast(w & jnp.uint32(_HI16), f32).astype(bf16))
        los.append(pltpu.bitcast(w << 16, f32).astype(bf16))
    acc = o_ref[...] + jnp.dot(jnp.concatenate(his, axis=1), va_ref[...], preferred_element_type=f32)
    acc = acc + jnp.dot(jnp.concatenate(los, axis=1), vb_ref[...], preferred_element_type=f32)

    if final_norm:
        last = e == pl.num_programs(1) - 1

        @pl.when(last)
        def _():
            o_ref[...] = _rms(acc, nw_ref[...])

        @pl.when(jnp.logical_not(last))
        def _():
            o_ref[...] = acc
    else:
        o_ref[...] = acc


def peer_out(i1, i2, c, v, x, final_nw=None):
    M, D = x.shape
    E = v.shape[0]
    S = PEER_SLOTS
    tm = _tile(M, 512)
    nr = 4
    te = nr * PEER_NKEYS
    ne = G_HALF // nr
    assert E == PEER_NKEYS * PEER_NKEYS and G_HALF % nr == 0
    tok = lambda i, e: (i, 0)
    nw = jnp.ones((D,), f32) if final_nw is None else final_nw
    body = functools.partial(_peer_out_body, tm=tm, nr=nr, final_norm=final_nw is not None)
    return pl.pallas_call(
        body,
        grid=(M // tm, ne),
        in_specs=[pl.BlockSpec((tm, S), tok), pl.BlockSpec((tm, S), tok), pl.BlockSpec((tm, S), tok),
                  pl.BlockSpec((te, D), lambda i, e: (e, 0)),
                  pl.BlockSpec((te, D), lambda i, e: (ne + e, 0)),
                  pl.BlockSpec((tm, D), tok),
                  pl.BlockSpec((1, D), lambda i, e: (0, 0))],
        out_specs=pl.BlockSpec((tm, D), tok),
        out_shape=jax.ShapeDtypeStruct((M, D), f32),
        scratch_shapes=[pltpu.VMEM((tm * G_PITCH, PEER_NKEYS), jnp.uint32)],
        compiler_params=_params("parallel", "arbitrary"),
        name="peer_out",
    )(i1, i2, c, v, v, x, nw.reshape(1, D))


def peer_ffn(x, nw, wq_t, keys, u, v, final_nw=None):
    M, D = x.shape
    Mp = -(-M // LANES) * LANES
    xp = jnp.pad(x, ((0, Mp - M), (0, 0))) if Mp != M else x
    xn, i1, i2, g = peer_route(xp, nw, wq_t, keys)
    c = peer_hidden(xn, u, i1, i2, g)
    out = peer_out(i1, i2, c, v, xp, final_nw)
    return out[:M] if Mp != M else out


def _pad_time(a, B, T, Tp):
    if Tp == T:
        return a
    W = a.shape[1]
    return jnp.pad(a.reshape(B, T, W), ((0, 0), (0, Tp - T), (0, 0))).reshape(B * Tp, W)


def _unpad_time(a, B, T, Tp):
    if Tp == T:
        return a
    W = a.shape[1]
    return a.reshape(B, Tp, W)[:, :T].reshape(B * T, W)


def _trunk(x, past, hgrn0, ssm0, conv0, wts):
    B, T, D = x.shape
    M = B * T
    H = FOX_HEADS
    if T % SEQ_CHUNK == 0:
        Tp = T
    else:
        assert T <= SEQ_CHUNK
        Tp = SEQ_CHUNK
    x2 = x.reshape(M, D)

    proj = norm_matmul(x2, wts["norm_mix"][0], wts["w_in_a"], "in_proj_a")
    ff = proj[:, 7 * FOX_W:7 * FOX_W + H]
    if past is None:
        ff_t = ff.reshape(B, T, H).transpose(0, 2, 1).reshape(B * H, T)
        lf_t, c_t = fox_gate(ff_t, jnp.tile(wts["b_fox_f"], B).reshape(B * H, 1))
        flogf = lf_t.reshape(B, H, T).transpose(0, 2, 1)
        c_bth = c_t.reshape(B, H, T).transpose(0, 2, 1)
        fo, fk, fv = fox_prompt_attn(proj, c_bth, c_t.reshape(B * H, 1, T), B, T)
        fk = fk.reshape(B, T, H, FOX_HD)
        fv = fv.reshape(B, T, H, FOX_HD)
    else:
        fk = proj[:, FOX_W:2 * FOX_W].reshape(B, T, H, FOX_HD)
        fv = proj[:, 2 * FOX_W:3 * FOX_W].reshape(B, T, H, FOX_HD)
        assert T * H <= LANES
        k_pool, v_pool, lf_pool, page_table = past
        n_phys = k_pool.shape[0]
        pad = LANES - T * H
        ff_l = jnp.pad(ff.reshape(B, T * H), ((0, 0), (0, pad)))
        lf_l, c_l = fox_gate(ff_l, jnp.tile(wts["b_fox_f"], LANES // H).reshape(1, LANES), stride=H)
        flogf = lf_l[:, :T * H].reshape(B, T, H)
        sfx_pool, tot_pool = page_suffix_sums(lf_pool.reshape(n_phys, PAGE * H), H)
        q_r = proj[:, 0:FOX_W].reshape(B, T * H, FOX_HD).astype(bf16)
        k_new = jnp.pad(proj[:, FOX_W:2 * FOX_W].reshape(B, T * H, FOX_HD), ((0, 0), (0, pad), (0, 0)))
        v_new = jnp.pad(proj[:, 2 * FOX_W:3 * FOX_W].reshape(B, T * H, FOX_HD), ((0, 0), (0, pad), (0, 0)))
        fo = fox_decode_attn(q_r, k_new, v_new, c_l.reshape(B, 1, LANES),
                             k_pool.reshape(n_phys, PAGE * H, FOX_HD), v_pool.reshape(n_phys, PAGE * H, FOX_HD),
                             sfx_pool.reshape(n_phys, 1, PAGE * H), tot_pool.reshape(n_phys, 1, PAGE * H),
                             page_table).reshape(M, FOX_W)

    if Tp == T:
        hsrc, hcb0 = proj, 3 * H
    else:
        hsrc, hcb0 = _pad_time(proj[:, 3 * FOX_W:3 * FOX_W + 4 * HGRN_W], B, T, Tp), 0
    ho, h_state = gla(hsrc, hcb0, wts["hgrn_lb_logits"], wts["hgrn_norm_w"], hgrn0, B, Tp, T, 0)
    ho = _unpad_time(ho, B, T, Tp)
    x2 = matmul_residual([fo, ho], wts["w_out_a"], x2, "out_proj_a")
    x2 = peer_ffn(x2, wts["norm_ffn"][0], wts["peer_wq_t"][0], wts["peer_keys"][0], wts["peer_u"][0],
                  wts["peer_v"][0])

    projc = norm_matmul(x2, wts["norm_mix"][1], wts["w_in_c"], "in_proj_c")
    projc3 = projc.reshape(B, T, projc.shape[1])
    if T >= SSD_CONV - 1:
        conv_state = projc3[:, T - (SSD_CONV - 1):, SSD_INNER:SSD_INNER + SSD_CONV_DIM]
    else:
        conv_state = jnp.concatenate([conv0, projc3[:, :, SSD_INNER:SSD_INNER + SSD_CONV_DIM]],
                                     axis=1)[:, -(SSD_CONV - 1):]
    csrc = _pad_time(projc, B, T, Tp)
    conv0_p = jnp.pad(conv0, ((0, 0), (8 - (SSD_CONV - 1), 0), (0, 0)))
    yg, s_state = ssd(csrc, wts["conv_w"], wts["conv_b"], conv0_p, wts["dt_bias8"], wts["a_log8"],
                      wts["d_skip8"], wts["ssd_norm_w"], ssm0.reshape(B, SSD_GROUPS, SSD_GW, SSD_N), B, Tp, T)
    x2 = matmul_residual([_unpad_time(yg, B, T, Tp)], wts["w_out_c"], x2, "out_proj_c")
    y = peer_ffn(x2, wts["norm_ffn"][1], wts["peer_wq_t"][1], wts["peer_keys"][1], wts["peer_u"][1],
                 wts["peer_v"][1], final_nw=wts["norm_final"]).reshape(B, T, D)
    return (y, fk[None], fv[None], flogf[None], h_state[None],
            s_state.reshape(B, SSD_HEADS, SSD_HD, SSD_N)[None], conv_state[None])


def _row8(a):
    return jnp.tile(jnp.pad(a.astype(f32), (0, LANES - a.shape[0]))[None, :], (8, 1))


def kernel(x_prompt, x_sample, cache_fox_k, cache_fox_v, cache_fox_logf, state_hgrn, state_ssm, state_conv, page_table, norm_mix, norm_ffn, norm_final, w_in_a, b_fox_f, hgrn_lb_logits, hgrn_norm_w, w_out_a, w_in_c, conv_w, conv_b, dt_bias, a_log, d_skip, ssd_norm_w, w_out_c, peer_wq, peer_keys, peer_u, peer_v):
    D = x_prompt.shape[-1]
    wa = w_in_a[0]
    n_main = 3 * FOX_W
    wa_r = jnp.concatenate(
        [wa[:, :n_main], wa[:, n_main + FOX_HEADS:],
         jnp.pad(wa[:, n_main:n_main + FOX_HEADS], ((0, 0), (0, N_TAIL - FOX_HEADS)))], axis=1).astype(bf16)
    wc = w_in_c[0]
    wc_r = jnp.pad(wc, ((0, 0), (0, N_TAIL - SSD_HEADS))).astype(bf16)
    wts = dict(
        norm_mix=norm_mix, norm_ffn=norm_ffn, norm_final=norm_final,
        w_in_a=wa_r, b_fox_f=b_fox_f[0], hgrn_lb_logits=hgrn_lb_logits, hgrn_norm_w=hgrn_norm_w[0],
        w_out_a=w_out_a[0].astype(bf16),
        w_in_c=wc_r, conv_w=conv_w[0], conv_b=conv_b[0],
        dt_bias8=_row8(dt_bias[0]), a_log8=_row8(a_log[0]), d_skip8=_row8(d_skip[0]),
        ssd_norm_w=ssd_norm_w[0], w_out_c=w_out_c[0].astype(bf16),
        peer_wq_t=jnp.swapaxes(peer_wq, 1, 2).astype(bf16), peer_keys=peer_keys.astype(bf16),
        peer_u=peer_u.astype(bf16), peer_v=peer_v.astype(bf16),
    )
    Bp = x_prompt.shape[0]
    hgrn0_p = jnp.zeros((Bp, HGRN_HEADS, HGRN_DK, HGRN_DV), f32)
    ssm0_p = jnp.zeros((Bp, SSD_HEADS, SSD_HD, SSD_N), f32)
    conv0_p = jnp.zeros((Bp, SSD_CONV - 1, SSD_CONV_DIM), f32)
    y_p, fk_p, fv_p, fl_p, h_p, s_p, c_p = _trunk(x_prompt, None, hgrn0_p, ssm0_p, conv0_p, wts)
    past = (cache_fox_k[0], cache_fox_v[0], cache_fox_logf[0], page_table)
    y_s, fk_s, fv_s, fl_s, h_s, s_s, c_s = _trunk(x_sample, past, state_hgrn[0], state_ssm[0], state_conv[0], wts)
    return (y_p, y_s, fk_p, fv_p, fl_p, fk_s, fv_s, fl_s, h_p, h_s, s_p, s_s, c_p, c_s)
```

```python
import functools
import math

import jax
import jax.numpy as jnp
from jax import lax
from jax.experimental import pallas as pl
from jax.experimental.pallas import tpu as pltpu

f32 = jnp.float32
bf16 = jnp.bfloat16
i32 = jnp.int32

EPS = 1e-6
LANES = 128
VMEM_LIMIT = 48 * 1024 * 1024
VMEM_LIMIT_LARGE = 56 * 1024 * 1024

FOX_HD = 128
FOX_HEADS = 8
FOX_W = FOX_HEADS * FOX_HD
HGRN_HEADS = 8
HGRN_DK = 128
HGRN_DV = 128
HGRN_W = HGRN_HEADS * HGRN_DK
SSD_HD = 64
SSD_HEADS = 64
SSD_GROUPS = 8
SSD_R = SSD_HEADS // SSD_GROUPS
SSD_N = 128
SSD_INNER = SSD_HEADS * SSD_HD
SSD_GW = SSD_INNER // SSD_GROUPS
SSD_CONV = 4
SSD_CONV_DIM = SSD_INNER + 2 * SSD_GROUPS * SSD_N
PEER_HEADS = 8
PEER_NKEYS = 128
PEER_DQ = 256
PEER_TOPK = 16
PEER_SLOTS = PEER_HEADS * PEER_TOPK
PAGE = 128
SEQ_CHUNK = 128
DIAG = 16
GLA_SUB = 64
GLA_SAFE_LOG = 80.0
N_TAIL = 512

_NT = (((1,), (1,)), ((), ()))
_TN = (((0,), (0,)), ((), ()))


def _params(*sem, vmem=VMEM_LIMIT):
    return pltpu.CompilerParams(dimension_semantics=sem, vmem_limit_bytes=vmem)


def _tile(n, target):
    t = min(n, target)
    while n % t:
        t -= 8
    return t


def _cumsum_axis(x, axis, stride=1):
    n = x.shape[axis]
    idx = lax.broadcasted_iota(i32, x.shape, axis)
    s = stride
    while s < n:
        x = x + jnp.where(idx >= s, pltpu.roll(x, s, axis=axis), 0.0)
        s *= 2
    return x


def _row_reduce(x, combine, reduce):
    acc = x[:, 0:LANES]
    for i in range(1, x.shape[1] // LANES):
        acc = combine(acc, x[:, i * LANES:(i + 1) * LANES])
    return reduce(acc, axis=1, keepdims=True)


def _rms(x, w):
    ms = jnp.mean(x * x, axis=-1, keepdims=True)
    return x * lax.rsqrt(ms + EPS) * w


def _norm_matmul_body(x_ref, nw_ref, w_ref, o_ref, xn_ref):
    @pl.when(pl.program_id(1) == 0)
    def _():
        xn_ref[...] = _rms(x_ref[...], nw_ref[...]).astype(bf16)

    o_ref[...] = jnp.dot(xn_ref[...], w_ref[...], preferred_element_type=f32)


def norm_matmul(x, nw, w, name):
    M, K = x.shape
    N = w.shape[1]
    tm, tn = _tile(M, 1024), _tile(N, 1536)
    return pl.pallas_call(
        _norm_matmul_body,
        grid=(M // tm, N // tn),
        in_specs=[pl.BlockSpec((tm, K), lambda i, j: (i, 0)),
                  pl.BlockSpec((1, K), lambda i, j: (0, 0)),
                  pl.BlockSpec((K, tn), lambda i, j: (0, j))],
        out_specs=pl.BlockSpec((tm, tn), lambda i, j: (i, j)),
        out_shape=jax.ShapeDtypeStruct((M, N), f32),
        scratch_shapes=[pltpu.VMEM((tm, K), bf16)],
        compiler_params=_params("parallel", "arbitrary"),
        name=name,
    )(x, nw.reshape(1, K), w)


def _matmul_res_body(*refs, n_parts):
    a_refs, w_ref, r_ref, o_ref = refs[:n_parts], refs[n_parts], refs[n_parts + 1], refs[n_parts + 2]
    acc = r_ref[...]
    k0 = 0
    for a_ref in a_refs:
        k1 = k0 + a_ref.shape[1]
        acc = acc + jnp.dot(a_ref[...].astype(bf16), w_ref[k0:k1, :], preferred_element_type=f32)
        k0 = k1
    o_ref[...] = acc


def matmul_residual(parts, w, res, name):
    M = parts[0].shape[0]
    K, N = w.shape
    tm, tn = _tile(M, 1024), _tile(N, 512)
    return pl.pallas_call(
        functools.partial(_matmul_res_body, n_parts=len(parts)),
        grid=(M // tm, N // tn),
        in_specs=([pl.BlockSpec((tm, a.shape[1]), lambda i, j: (i, 0)) for a in parts]
                  + [pl.BlockSpec((K, tn), lambda i, j: (0, j)),
                     pl.BlockSpec((tm, tn), lambda i, j: (i, j))]),
        out_specs=pl.BlockSpec((tm, tn), lambda i, j: (i, j)),
        out_shape=jax.ShapeDtypeStruct((M, N), f32),
        compiler_params=_params("parallel", "arbitrary"),
        name=name,
    )(*parts, w, res)


def _fox_gate_body(ff_ref, b_ref, lf_ref, c_ref, *, stride):
    lf = jax.nn.log_sigmoid(ff_ref[...] + b_ref[...])
    lf_ref[...] = lf
    c_ref[...] = _cumsum_axis(lf, 1, stride)


def fox_gate(ff, bias, stride=1):
    R, T = ff.shape
    shp = jax.ShapeDtypeStruct((R, T), f32)
    return pl.pallas_call(functools.partial(_fox_gate_body, stride=stride), out_shape=[shp, shp], name="fox_gate",
                          compiler_params=pltpu.CompilerParams(vmem_limit_bytes=VMEM_LIMIT))(ff, bias)


def _fox_attn_body(q_ref, k_ref, v_ref, ct_ref, cs_ref, o_ref, ko_ref, vo_ref, vt_sc, cs_sc, m_sc, l_sc, acc_sc,
                   *, tq, scale, hpb):
    hg = pl.program_id(1)
    qi = pl.program_id(2)
    D = FOX_HD
    T = k_ref.shape[0]

    @pl.when(qi == 0)
    def _():
        ko_ref[...] = k_ref[...]
        vo_ref[...] = v_ref[...]
        cs = cs_ref[0]
        lane = lax.broadcasted_iota(i32, cs.shape, 1)
        for j in range(hpb):
            col = jnp.sum(jnp.where(lane == hg * hpb + j, cs, 0.0), axis=1, keepdims=True)
            cs_sc[j] = jnp.broadcast_to(col, (T, LANES))
            for t in range(T // LANES):
                blk = v_ref[t * LANES:(t + 1) * LANES, j * D:(j + 1) * D]
                vt_sc[j, :, t * LANES:(t + 1) * LANES] = blk.T.astype(bf16)

    q0 = pl.multiple_of(qi * tq, tq)
    qs, cts = [], []
    for j in range(hpb):
        qs.append(q_ref[:, j * D:(j + 1) * D].astype(bf16))
        cts.append(ct_ref[j, :, pl.ds(q0, tq)])
        m_sc[j] = jnp.full((1, tq), -jnp.inf, f32)
        l_sc[j] = jnp.zeros((1, tq), f32)
        acc_sc[j] = jnp.zeros((D, tq), f32)
    causal = lax.broadcasted_iota(i32, (tq, tq), 0) <= lax.broadcasted_iota(i32, (tq, tq), 1)

    def block(ki, masked):
        start = pl.multiple_of(ki * tq, tq)
        for j in range(hpb):
            k = k_ref[pl.ds(start, tq), j * D:(j + 1) * D].astype(bf16)
            st = lax.dot_general(k, qs[j], _NT, preferred_element_type=f32) * scale
            csb = cs_sc[j, pl.ds(start, tq), :]
            st = st + cts[j] - jnp.concatenate([csb] * (tq // LANES), axis=1)
            if masked:
                st = jnp.where(causal, st, -jnp.inf)
            m_prev = m_sc[j]
            m_new = jnp.maximum(m_prev, jnp.max(st, axis=0, keepdims=True))
            alpha = jnp.exp(m_prev - m_new)
            p = jnp.exp(st - m_new)
            l_sc[j] = alpha * l_sc[j] + jnp.sum(p, axis=0, keepdims=True)
            acc_sc[j] = alpha * acc_sc[j] + jnp.dot(vt_sc[j, :, pl.ds(start, tq)], p.astype(bf16),
                                                    preferred_element_type=f32)
            m_sc[j] = m_new

    def body(ki, carry):
        block(ki, False)
        return carry

    lax.fori_loop(0, qi, body, 0)
    block(qi, True)
    for j in range(hpb):
        o_t = acc_sc[j] / l_sc[j]
        for t in range(tq // LANES):
            o_ref[t * LANES:(t + 1) * LANES, j * D:(j + 1) * D] = o_t[:, t * LANES:(t + 1) * LANES].T


def fox_prompt_attn(proj, c_bth, c_rows, B, T):
    tq = _tile(T, 256)
    nq = T // tq
    H = FOX_HEADS
    hpb = 4
    W = hpb * FOX_HD
    G = H // hpb
    body = functools.partial(_fox_attn_body, tq=tq, scale=FOX_HD ** -0.5, hpb=hpb)
    return pl.pallas_call(
        body,
        grid=(B, G, nq),
        in_specs=[pl.BlockSpec((tq, W), lambda b, g, qi: (b * nq + qi, g)),
                  pl.BlockSpec((T, W), lambda b, g, qi: (b, G + g)),
                  pl.BlockSpec((T, W), lambda b, g, qi: (b, 2 * G + g)),
                  pl.BlockSpec((hpb, 1, T), lambda b, g, qi: (b * G + g, 0, 0)),
                  pl.BlockSpec((1, T, H), lambda b, g, qi: (b, 0, 0))],
        out_specs=[pl.BlockSpec((tq, W), lambda b, g, qi: (b * nq + qi, g)),
                   pl.BlockSpec((T, W), lambda b, g, qi: (b, g)),
                   pl.BlockSpec((T, W), lambda b, g, qi: (b, g))],
        out_shape=[jax.ShapeDtypeStruct((B * T, FOX_W), f32)] * 3,
        scratch_shapes=[pltpu.VMEM((hpb, FOX_HD, T), bf16), pltpu.VMEM((hpb, T, LANES), f32),
                        pltpu.VMEM((hpb, 1, tq), f32), pltpu.VMEM((hpb, 1, tq), f32),
                        pltpu.VMEM((hpb, FOX_HD, tq), f32)],
        compiler_params=_params("parallel", "parallel", "arbitrary"),
        name="fox_prompt_attn",
    )(proj, proj, proj, c_rows, c_bth)


def _suffix_body(x_ref, sfx_ref, tot_ref, *, stride):
    x = x_ref[...]
    lane = lax.broadcasted_iota(i32, x.shape, 1)
    n = x.shape[1]
    y = jnp.where(lane < n - stride, pltpu.roll(x, n - stride, axis=1), 0.0)
    t = x
    s = stride
    while s < n:
        y = y + jnp.where(lane < n - s, pltpu.roll(y, n - s, axis=1), 0.0)
        t = t + pltpu.roll(t, s, axis=1)
        s *= 2
    sfx_ref[...] = y
    tot_ref[...] = t


def page_suffix_sums(lf_rows, stride):
    R, n = lf_rows.shape
    tr = _tile(R, 512)
    shp = jax.ShapeDtypeStruct((R, n), f32)
    return pl.pallas_call(
        functools.partial(_suffix_body, stride=stride),
        grid=(R // tr,),
        in_specs=[pl.BlockSpec((tr, n), lambda i: (i, 0))],
        out_specs=[pl.BlockSpec((tr, n), lambda i: (i, 0)), pl.BlockSpec((tr, n), lambda i: (i, 0))],
        out_shape=[shp, shp],
        compiler_params=_params("parallel"),
        name="fox_page_suffix",
    )(lf_rows)


def _fox_dec_body(pt_ref, q_ref, kn_ref, vn_ref, cn_ref, *refs, scale, G):
    del pt_ref
    kp_refs, vp_refs = refs[0:G], refs[G:2 * G]
    sfx_refs, tot_refs = refs[2 * G:3 * G], refs[3 * G:4 * G]
    o_ref, m_sc, l_sc, acc_sc, carry_sc, cc_sc = refs[4 * G:]
    H = FOX_HEADS
    j = pl.program_id(1)
    q = q_ref[0]

    def partial_softmax(k, v, bias):
        s = lax.dot_general(q, k.astype(bf16), _NT, preferred_element_type=f32) * scale + bias
        m = _row_reduce(s, jnp.maximum, jnp.max)
        p = jnp.exp(s - m)
        return m, _row_reduce(p, jnp.add, jnp.sum), jnp.dot(p.astype(bf16), v.astype(bf16),
                                                            preferred_element_type=f32)

    def merge(parts):
        m_prev = m_sc[...]
        m_new = m_prev
        for m, _, _ in parts:
            m_new = jnp.maximum(m_new, m)
        alpha = jnp.exp(m_prev - m_new)
        l = alpha * l_sc[...]
        acc = alpha * acc_sc[...]
        for m, lp, ap in parts:
            w = jnp.exp(m - m_new)
            l = l + w * lp
            acc = acc + w * ap
        m_sc[...] = m_new
        l_sc[...] = l
        acc_sc[...] = acc

    def step(k, v, bias):
        merge([partial_softmax(k, v, bias)])

    @pl.when(j == 0)
    def _():
        m_sc[...] = jnp.full(m_sc.shape, -jnp.inf, f32)
        l_sc[...] = jnp.zeros(l_sc.shape, f32)
        acc_sc[...] = jnp.zeros(acc_sc.shape, f32)
        carry_sc[...] = jnp.zeros(carry_sc.shape, f32)
        cn = cn_ref[0]
        shape = (q.shape[0], cn.shape[1])
        row = lax.broadcasted_iota(i32, shape, 0)
        col = lax.broadcasted_iota(i32, shape, 1)
        cc = jnp.sum(jnp.where(row == col, cn, 0.0), axis=1, keepdims=True)
        cc_sc[...] = cc
        ok = jnp.logical_and(col % H == row % H, col // H <= row // H)
        step(kn_ref[0], vn_ref[0], jnp.where(ok, cc - cn, -jnp.inf))

    @pl.when(j > 0)
    def _():
        carry = carry_sc[...]
        shape = (q.shape[0], carry.shape[1])
        same_head = lax.broadcasted_iota(i32, shape, 1) % H == lax.broadcasted_iota(i32, shape, 0) % H
        cc = cc_sc[...]
        parts = []
        for g in range(G):
            bias = jnp.where(same_head, cc + (carry + sfx_refs[g][0]), -jnp.inf)
            parts.append(partial_softmax(kp_refs[g][0], vp_refs[g][0], bias))
            carry = carry + tot_refs[g][0]
        merge(parts)
        carry_sc[...] = carry

    @pl.when(j == pl.num_programs(1) - 1)
    def _():
        o_ref[0] = acc_sc[...] / l_sc[...]


def fox_decode_attn(q, k_new, v_new, c_new, k_pool, v_pool, sfx_pool, tot_pool, page_table):
    Bd, n_pages = page_table.shape
    R, D = q.shape[1], q.shape[2]
    PW = k_pool.shape[1]
    G = math.gcd(n_pages, 8)

    def page(g):
        return lambda b, j, pt: (pt[b, n_pages - 1 - (jnp.maximum(j, 1) - 1) * G - g], 0, 0)

    same = lambda b, j, pt: (b, 0, 0)
    grid_spec = pltpu.PrefetchScalarGridSpec(
        num_scalar_prefetch=1,
        grid=(Bd, n_pages // G + 1),
        in_specs=([pl.BlockSpec((1, R, D), same),
                   pl.BlockSpec((1, LANES, D), same),
                   pl.BlockSpec((1, LANES, D), same),
                   pl.BlockSpec((1, 1, LANES), same)]
                  + [pl.BlockSpec((1, PW, D), page(g)) for g in range(G)]
                  + [pl.BlockSpec((1, PW, D), page(g)) for g in range(G)]
                  + [pl.BlockSpec((1, 1, PW), page(g)) for g in range(G)]
                  + [pl.BlockSpec((1, 1, PW), page(g)) for g in range(G)]),
        out_specs=pl.BlockSpec((1, R, D), same),
        scratch_shapes=[pltpu.VMEM((R, 1), f32), pltpu.VMEM((R, 1), f32), pltpu.VMEM((R, D), f32),
                        pltpu.VMEM((1, PW), f32), pltpu.VMEM((R, 1), f32)],
    )
    body = functools.partial(_fox_dec_body, scale=FOX_HD ** -0.5, G=G)
    return pl.pallas_call(
        body,
        grid_spec=grid_spec,
        out_shape=jax.ShapeDtypeStruct((Bd, R, D), f32),
        compiler_params=_params("parallel", "arbitrary"),
        name="fox_decode_attn",
    )(page_table, q, k_new, v_new, c_new, *([k_pool] * G), *([v_pool] * G), *([sfx_pool] * G),
      *([tot_pool] * G))


def _gla_body(q_ref, f_ref, i_ref, g_ref, lbl_ref, nw_ref, s0_ref, o_ref, so_ref,
              s_sc, q_sc, k_sc, b_sc, v_sc, oa_sc, *, C, t_valid, layer_j, HB):
    c = pl.program_id(2)
    SUB = GLA_SUB
    K = HGRN_DK

    @pl.when(c == 0)
    def _():
        for j in range(HB):
            s_sc[j] = s0_ref[0, j].T

    lg = lbl_ref[...]
    e = jnp.exp(lg - jnp.max(lg, axis=0, keepdims=True))
    sm = e / jnp.sum(e, axis=0, keepdims=True)
    lb = jnp.sum(sm[0:layer_j + 1], axis=0, keepdims=True)

    z = f_ref[...]
    hlogf = jnp.log(lb + (1.0 - lb) * jax.nn.sigmoid(z))
    hk = (1.0 - lb) * jax.nn.sigmoid(-z)
    valid = (c * C + lax.broadcasted_iota(i32, (C, 1), 0)) < t_valid
    lc = jnp.where(valid, hlogf, 0.0)
    kk = jnp.where(valid, hk, 0.0)
    subs = list(range(0, C, SUB))
    bs = [_cumsum_axis(lc[lo:lo + SUB, :], 0) for lo in subs]
    bl_min = bs[0][SUB - 1:SUB, :]
    for b_ in bs[1:]:
        bl_min = jnp.minimum(bl_min, b_[SUB - 1:SUB, :])
    in_range = jnp.min(bl_min) >= -GLA_SAFE_LOG
    tril = lax.broadcasted_iota(i32, (SUB, SUB), 1) <= lax.broadcasted_iota(i32, (SUB, SUB), 0)

    @pl.when(in_range)
    def _():
        for j in range(HB):
            cols = slice(j * K, (j + 1) * K)
            st = s_sc[j]
            for si, lo in enumerate(subs):
                rows = slice(lo, lo + SUB)
                b = bs[si][:, cols]
                bl = b[SUB - 1:SUB, :]
                ks = kk[rows, cols]
                vs = i_ref[rows, cols].astype(bf16)
                qd = (q_ref[rows, cols] * jnp.exp(b)).astype(bf16)
                ki = (ks * jnp.exp(-b)).astype(bf16)
                a = jnp.where(tril, lax.dot_general(qd, ki, _NT, preferred_element_type=f32), 0.0)
                o = (lax.dot_general(qd, st.astype(bf16), _NT, preferred_element_type=f32)
                     + jnp.dot(a.astype(bf16), vs, preferred_element_type=f32))
                kd = (ks * jnp.exp(bl - b)).astype(bf16)
                st = st * jnp.exp(bl) + lax.dot_general(vs, kd, _TN, preferred_element_type=f32)
                o_ref[rows, cols] = _rms(o, nw_ref[...]) * jax.nn.silu(g_ref[rows, cols])
            s_sc[j] = st

    def off_diag(lo, hi):
        if hi - lo <= DIAG:
            return
        mid = (lo + hi) // 2
        r = b_sc[pl.ds(mid - 1, 1), :]
        qh = q_sc[mid:hi, :] * jnp.exp(b_sc[mid:hi, :] - r)
        kh = k_sc[lo:mid, :] * jnp.exp(r - b_sc[lo:mid, :])
        a = lax.dot_general(qh.astype(bf16), kh.astype(bf16), _NT, preferred_element_type=f32)
        oa_sc[mid:hi, :] += jnp.dot(a.astype(bf16), v_sc[lo:mid, :].astype(bf16), preferred_element_type=f32)
        off_diag(lo, mid)
        off_diag(mid, hi)

    srow = lax.broadcasted_iota(i32, (DIAG, 1), 0)

    def diag(i, carry):
        lo = pl.multiple_of(i * DIAG, DIAG)
        qb = q_sc[pl.ds(lo, DIAG), :]
        kb = k_sc[pl.ds(lo, DIAG), :]
        bb = b_sc[pl.ds(lo, DIAG), :]
        vb = v_sc[pl.ds(lo, DIAG), :]
        ob = jnp.zeros((DIAG, vb.shape[1]), f32)
        for t in range(DIAG):
            dec = jnp.exp(jnp.where(srow <= t, bb[t:t + 1, :] - bb, -jnp.inf))
            w = jnp.sum(qb[t:t + 1, :] * kb * dec, axis=1, keepdims=True)
            ot = jnp.sum(w * vb, axis=0, keepdims=True)
            ob = jnp.where(srow == t, ot, ob)
        oa_sc[pl.ds(lo, DIAG), :] += ob
        return carry

    @pl.when(jnp.logical_not(in_range))
    def _():
        for j in range(HB):
            cols = slice(j * K, (j + 1) * K)
            q_sc[...] = q_ref[:, cols]
            k_sc[...] = kk[:, cols]
            v_sc[...] = i_ref[:, cols]
            for si, lo in enumerate(subs):
                rows = slice(lo, lo + SUB)
                b = bs[si][:, cols]
                b_sc[rows, :] = b
                bl = b[SUB - 1:SUB, :]
                st = s_sc[j]
                qd = (q_sc[rows, :] * jnp.exp(b)).astype(bf16)
                oa_sc[rows, :] = lax.dot_general(qd, st.astype(bf16), _NT, preferred_element_type=f32)
                off_diag(lo, lo + SUB)
                lax.fori_loop(lo // DIAG, (lo + SUB) // DIAG, diag, 0)
                kd = (k_sc[rows, :] * jnp.exp(bl - b)).astype(bf16)
                s_sc[j] = st * jnp.exp(bl) + lax.dot_general(v_sc[rows, :].astype(bf16), kd, _TN,
                                                             preferred_element_type=f32)
            o_ref[:, cols] = _rms(oa_sc[...], nw_ref[...]) * jax.nn.silu(g_ref[:, cols])

    @pl.when(c == pl.num_programs(2) - 1)
    def _():
        for j in range(HB):
            so_ref[0, j] = s_sc[j].T


def gla(src, cb0, lb_logits, norm_w, s0, B, Tp, t_valid, layer_j):
    H, C = HGRN_HEADS, SEQ_CHUNK
    HB = 4
    G = H // HB
    W = HB * HGRN_DK
    nc = Tp // C
    L = lb_logits.shape[0]
    assert cb0 % HB == 0

    def col(g):
        return pl.BlockSpec((C, W), lambda b, h, c: (b * nc + c, (cb0 + g * H) // HB + h))

    body = functools.partial(_gla_body, C=C, t_valid=t_valid, layer_j=layer_j, HB=HB)
    return pl.pallas_call(
        body,
        grid=(B, G, nc),
        in_specs=[col(0), col(1), col(2), col(3),
                  pl.BlockSpec((L, W), lambda b, h, c: (0, h)),
                  pl.BlockSpec((1, HGRN_DV), lambda b, h, c: (0, 0)),
                  pl.BlockSpec((1, HB, HGRN_DK, HGRN_DV), lambda b, h, c: (b, h, 0, 0))],
        out_specs=[pl.BlockSpec((C, W), lambda b, h, c: (b * nc + c, h)),
                   pl.BlockSpec((1, HB, HGRN_DK, HGRN_DV), lambda b, h, c: (b, h, 0, 0))],
        out_shape=[jax.ShapeDtypeStruct((B * Tp, HGRN_W), f32),
                   jax.ShapeDtypeStruct((B, H, HGRN_DK, HGRN_DV), f32)],
        scratch_shapes=[pltpu.VMEM((HB, HGRN_DV, HGRN_DK), f32)] + [pltpu.VMEM((C, HGRN_DK), f32)] * 5,
        compiler_params=_params("parallel", "parallel", "arbitrary"),
        name="hgrn2_gla",
    )(src, src, src, src, lb_logits, norm_w.reshape(1, HGRN_DV), s0)


def _causal_conv_silu(x_ref, w_ref, b_ref, c0_ref, xp_sc, first, C, rows):
    @pl.when(first)
    def _():
        xp_sc[0:8, :] = c0_ref[0]

    xp_sc[8:8 + rows, :] = x_ref[0:rows, :]
    w = w_ref[...]
    acc = b_ref[...] + w[0:1, :] * xp_sc[pl.ds(8 - (SSD_CONV - 1), rows), :]
    for i in range(1, SSD_CONV):
        acc = acc + w[i:i + 1, :] * xp_sc[pl.ds(8 - (SSD_CONV - 1) + i, rows), :]
    if rows == C:
        xp_sc[0:8, :] = xp_sc[C:C + 8, :]
    return jax.nn.silu(acc)


def _ssd_body(z_ref, x_ref, bm_ref, cm_ref, dt_ref, wx_ref, wb_ref, wc_ref, bx_ref, bb_ref, bc_ref,
              c0x_ref, c0b_ref, c0c_ref, dtb_ref, alog_ref, dsk_ref, nw_ref, s0_ref,
              y_ref, so_ref, st_sc, xpx_sc, xpb_sc, xpc_sc, *, C, t_valid, rows):
    g = pl.program_id(1)
    c = pl.program_id(2)
    R = SSD_R
    Ru = rows
    npair = SSD_GW // LANES

    @pl.when(c == 0)
    def _():
        s0 = s0_ref[0, 0]
        for i in range(npair):
            st_sc[:, i * LANES:(i + 1) * LANES] = s0[i * LANES:(i + 1) * LANES, :].T

    xs = _causal_conv_silu(x_ref, wx_ref, bx_ref, c0x_ref, xpx_sc, c == 0, C, Ru)
    bm = _causal_conv_silu(bm_ref, wb_ref, bb_ref, c0b_ref, xpb_sc, c == 0, C, Ru).astype(bf16)
    cm = _causal_conv_silu(cm_ref, wc_ref, bc_ref, c0c_ref, xpc_sc, c == 0, C, Ru).astype(bf16)

    shift = (LANES - g * R) % LANES
    dt_all = jax.nn.softplus(dt_ref[...] + dtb_ref[0:1, :])
    valid = (c * C + lax.broadcasted_iota(i32, (C, 1), 0)) < t_valid
    dt_all = jnp.where(valid, dt_all, 0.0)
    la = dt_all * (-jnp.exp(alog_ref[0:1, :]))
    cs_all = _cumsum_axis(la, 0)
    cs_full = pltpu.roll(cs_all, shift, axis=1)
    dt_full = pltpu.roll(dt_all, shift, axis=1)
    dsk8 = pltpu.roll(dsk_ref[...], shift, axis=1)
    cs_t = cs_full.T[:, 0:Ru]
    dt_t = dt_full.T[:, 0:Ru]
    cs = cs_full[0:Ru, :]
    dt = dt_full[0:Ru, :]
    cl = cs_full[C - 1:C, :]
    ecs = jnp.exp(cs)
    wst = dt * jnp.exp(cl - cs)
    ecl8 = jnp.exp(cs_full[C - 8:C, :])

    cb = lax.dot_general(cm, bm, _NT, preferred_element_type=f32)
    causal = lax.broadcasted_iota(i32, (Ru, Ru), 0) >= lax.broadcasted_iota(i32, (Ru, Ru), 1)
    low = lax.broadcasted_iota(i32, (Ru, LANES), 1) < SSD_HD

    def expand(a):
        lw = low[0:a.shape[0], :]
        return jnp.concatenate(
            [jnp.where(lw, a[:, 2 * p:2 * p + 1], a[:, 2 * p + 1:2 * p + 2]) for p in range(npair)], axis=1)

    ys = []
    for p in range(npair):
        xp = xs[:, p * LANES:(p + 1) * LANES]
        acc = None
        for hh in range(2):
            r = 2 * p + hh
            dec = jnp.exp(jnp.where(causal, cs[:, r:r + 1] - cs_t[r:r + 1, :], -jnp.inf))
            wm = (cb * dec * dt_t[r:r + 1, :]).astype(bf16)
            xm = jnp.where(low if hh == 0 else jnp.logical_not(low), xp, 0.0).astype(bf16)
            part = jnp.dot(wm, xm, preferred_element_type=f32)
            acc = part if acc is None else acc + part
        ys.append(acc)
    y = jnp.concatenate(ys, axis=1)

    st = st_sc[...]
    y = y + jnp.dot(cm, st.astype(bf16), preferred_element_type=f32) * expand(ecs)
    y = y + expand(dsk8)[0:1, :] * xs

    xw = (xs * expand(wst)).astype(bf16)
    st_new = st * expand(ecl8)[7:8, :] + lax.dot_general(bm, xw, _TN, preferred_element_type=f32)
    st_sc[...] = st_new

    yg = y * jax.nn.silu(z_ref[0:Ru, :])
    y_ref[0:Ru, :] = _rms(yg, nw_ref[...]).astype(y_ref.dtype)
    if Ru < C:
        y_ref[Ru:C, :] = jnp.zeros((C - Ru, y_ref.shape[1]), y_ref.dtype)

    @pl.when(c == pl.num_programs(2) - 1)
    def _():
        for i in range(npair):
            so_ref[0, 0, i * LANES:(i + 1) * LANES, :] = st_new[:, i * LANES:(i + 1) * LANES].T


def ssd(src, conv_w, conv_b, conv0_p, dt_bias8, a_log8, d_skip8, norm_w, s0, B, Tp, t_valid):
    G, C = SSD_GROUPS, SEQ_CHUNK
    nc = Tp // C
    GW, N = SSD_GW, SSD_N
    dt_blk = (SSD_INNER + SSD_CONV_DIM) // LANES
    x0 = SSD_INNER // GW
    b0 = 2 * SSD_INNER // N
    wb0 = SSD_INNER // N
    rows = min(C, -(-t_valid // 16) * 16) if nc == 1 else C
    body = functools.partial(_ssd_body, C=C, t_valid=t_valid, rows=rows)
    par = lambda b, g, c: (0, 0)
    row = lambda off: (lambda b, g, c: (b * nc + c, off + g))
    wcol = lambda off: (lambda b, g, c: (0, off + g))
    c0col = lambda off: (lambda b, g, c: (b, 0, off + g))
    return pl.pallas_call(
        body,
        grid=(B, G, nc),
        in_specs=[pl.BlockSpec((C, GW), row(0)),
                  pl.BlockSpec((C, GW), row(x0)),
                  pl.BlockSpec((C, N), row(b0)),
                  pl.BlockSpec((C, N), row(b0 + G)),
                  pl.BlockSpec((C, LANES), lambda b, g, c: (b * nc + c, dt_blk)),
                  pl.BlockSpec((SSD_CONV, GW), wcol(0)),
                  pl.BlockSpec((SSD_CONV, N), wcol(wb0)),
                  pl.BlockSpec((SSD_CONV, N), wcol(wb0 + G)),
                  pl.BlockSpec((1, GW), wcol(0)),
                  pl.BlockSpec((1, N), wcol(wb0)),
                  pl.BlockSpec((1, N), wcol(wb0 + G)),
                  pl.BlockSpec((1, 8, GW), c0col(0)),
                  pl.BlockSpec((1, 8, N), c0col(wb0)),
                  pl.BlockSpec((1, 8, N), c0col(wb0 + G)),
                  pl.BlockSpec((8, LANES), par), pl.BlockSpec((8, LANES), par), pl.BlockSpec((8, LANES), par),
                  pl.BlockSpec((1, GW), lambda b, g, c: (0, g)),
                  pl.BlockSpec((1, 1, GW, N), lambda b, g, c: (b, g, 0, 0))],
        out_specs=[pl.BlockSpec((C, GW), lambda b, g, c: (b * nc + c, g)),
                   pl.BlockSpec((1, 1, GW, N), lambda b, g, c: (b, g, 0, 0))],
        out_shape=[jax.ShapeDtypeStruct((B * Tp, SSD_INNER), bf16),
                   jax.ShapeDtypeStruct((B, G, GW, N), f32)],
        scratch_shapes=[pltpu.VMEM((N, GW), f32), pltpu.VMEM((C + 8, GW), f32), pltpu.VMEM((C + 8, N), f32),
                        pltpu.VMEM((C + 8, N), f32)],
        compiler_params=_params("parallel", "parallel", "arbitrary"),
        name="ssd_scan",
    )(src, src, src, src, src, conv_w, conv_w, conv_w, conv_b.reshape(1, SSD_CONV_DIM),
      conv_b.reshape(1, SSD_CONV_DIM), conv_b.reshape(1, SSD_CONV_DIM), conv0_p, conv0_p, conv0_p,
      dt_bias8, a_log8, d_skip8, norm_w.reshape(1, SSD_INNER), s0)


def _sort_network(n):
    size = 1
    while size < n:
        size *= 2
    pairs = []
    p = 1
    while p < size:
        k = p
        while k >= 1:
            for j in range(k % p, size - k, 2 * k):
                for i in range(min(k, size - j - k)):
                    if (i + j) // (2 * p) == (i + j + k) // (2 * p):
                        pairs.append((i + j, i + j + k))
            k //= 2
        p *= 2
    return [(a, b) for a, b in pairs if b < n]


def _topk_sorted(s, k):
    n_rows, n = s.shape
    sl = 8
    nv = n_rows // sl
    vals = [s[sl * v:sl * (v + 1), :] for v in range(nv)]
    ids = [jnp.full((sl, n), v, i32) for v in range(nv)]
    for a, b in _sort_network(nv):
        va, vb, ia, ib = vals[a], vals[b], ids[a], ids[b]
        swap = (vb > va) | ((vb == va) & (ib < ia))
        vals[a], vals[b] = jnp.where(swap, vb, va), jnp.where(swap, va, vb)
        ids[a], ids[b] = jnp.where(swap, ib, ia), jnp.where(swap, ia, ib)
    sub = lax.broadcasted_iota(i32, (sl, n), 0)
    kio = lax.broadcasted_iota(i32, (k, n), 0)
    out_v = jnp.zeros((k, n), f32)
    out_i = jnp.zeros((k, n), i32)
    neg = jnp.full((sl, n), -jnp.inf, f32)
    for it in range(k):
        rid = ids[0] * sl + sub
        m = jnp.max(vals[0], axis=0, keepdims=True)
        idx = jnp.min(jnp.where(vals[0] == m, rid, n_rows), axis=0, keepdims=True)
        out_v = jnp.where(kio == it, m, out_v)
        out_i = jnp.where(kio == it, idx, out_i)
        sel = rid == idx
        for p in range(min(nv, k - 1 - it)):
            nxt_v = vals[p + 1] if p + 1 < nv else neg
            nxt_i = ids[p + 1] if p + 1 < nv else ids[p]
            vals[p] = jnp.where(sel, nxt_v, vals[p])
            ids[p] = jnp.where(sel, nxt_i, ids[p])
    return out_v, out_i


_CAND_HALF = PEER_TOPK // 2


def _cand_rows(v0, v1):
    K, Hh = PEER_TOPK, _CAND_HALF
    return jnp.concatenate([v0[0:1, :] + v1] + [v0[p:p + 1, :] + v1[0:Hh, :] for p in range(1, Hh)]
                           + [v0[Hh:K, :] + v1[0:1, :]], axis=0)


def _cand_pq(pos):
    K, Hh = PEER_TOPK, _CAND_HALF
    mid = pos - K
    tail = K + Hh * (Hh - 1)
    p = jnp.where(pos < K, 0, jnp.where(pos < tail, 1 + mid // Hh, Hh + (pos - tail)))
    q = jnp.where(pos < K, pos, jnp.where(pos < tail, mid % Hh, 0))
    return p, q


def _peer_route_body(x_ref, nw_ref, wqt_ref, keys_ref, xn_ref, i1_ref, i2_ref, g_ref,
                     qt_sc, i1t_sc, i2t_sc, gt_sc, *, tm):
    K = PEER_TOPK
    xn = _rms(x_ref[...], nw_ref[...]).astype(bf16)
    xn_ref[...] = xn
    qt_sc[...] = lax.dot_general(wqt_ref[...], xn, _NT, preferred_element_type=f32)

    def head(h, carry):
        base = pl.multiple_of(h * PEER_DQ, PEER_DQ)
        rows = pl.ds(pl.multiple_of(h * K, K), K)
        for half in range(tm // LANES):
            cols = slice(half * LANES, (half + 1) * LANES)
            sts = []
            for part in range(2):
                qp = qt_sc[pl.ds(base + part * (PEER_DQ // 2), PEER_DQ // 2), cols].astype(bf16)
                sts.append(jnp.dot(keys_ref[h, part], qp, preferred_element_type=f32))
            v0, i0 = _topk_sorted(sts[0], K)
            v1, i1 = _topk_sorted(sts[1], K)
            top, pos = _topk_sorted(_cand_rows(v0, v1), K)
            pp, qq = _cand_pq(pos)
            i1s = jnp.zeros((K, LANES), i32)
            i2s = jnp.zeros((K, LANES), i32)
            for r in range(K):
                i1s = jnp.where(pp == r, i0[r:r + 1, :], i1s)
                i2s = jnp.where(qq == r, i1[r:r + 1, :], i2s)
            e = jnp.exp(top - jnp.max(top, axis=0, keepdims=True))
            i1t_sc[rows, cols] = i1s
            i2t_sc[rows, cols] = i2s
            gt_sc[rows, cols] = e / jnp.sum(e, axis=0, keepdims=True)
        return carry

    lax.fori_loop(0, PEER_HEADS, head, 0)
    i1_ref[...] = i1t_sc[...].T
    i2_ref[...] = i2t_sc[...].T
    g_ref[...] = gt_sc[...].T


def peer_route(x, nw, wq_t, keys):
    M, D = x.shape
    tm = 256 if M % 256 == 0 else LANES
    HQ = PEER_HEADS * PEER_DQ
    S = PEER_SLOTS
    body = functools.partial(_peer_route_body, tm=tm)
    return pl.pallas_call(
        body,
        grid=(M // tm,),
        in_specs=[pl.BlockSpec((tm, D), lambda i: (i, 0)),
                  pl.BlockSpec((1, D), lambda i: (0, 0)),
                  pl.BlockSpec((HQ, D), lambda i: (0, 0)),
                  pl.BlockSpec((PEER_HEADS, 2, PEER_NKEYS, PEER_DQ // 2), lambda i: (0, 0, 0, 0))],
        out_specs=[pl.BlockSpec((tm, D), lambda i: (i, 0)),
                   pl.BlockSpec((tm, S), lambda i: (i, 0)),
                   pl.BlockSpec((tm, S), lambda i: (i, 0)),
                   pl.BlockSpec((tm, S), lambda i: (i, 0))],
        out_shape=[jax.ShapeDtypeStruct((M, D), bf16), jax.ShapeDtypeStruct((M, S), i32),
                   jax.ShapeDtypeStruct((M, S), i32), jax.ShapeDtypeStruct((M, S), f32)],
        scratch_shapes=[pltpu.VMEM((HQ, tm), f32), pltpu.VMEM((S, tm), i32), pltpu.VMEM((S, tm), i32),
                        pltpu.VMEM((S, tm), f32)],
        compiler_params=_params("parallel"),
        name="peer_route",
    )(x, nw.reshape(1, D), wq_t, keys)


def _peer_hidden_body(xn_ref, u_ref, i1_ref, i2_ref, g_ref, c_ref, h_sc, s_sc, *, te):
    e = pl.program_id(1)

    @pl.when(e == 0)
    def _():
        h_sc[...] = jnp.zeros(h_sc.shape, f32)
        s_sc[1] = jnp.zeros(s_sc.shape[1:], f32)

    slot = e % 2
    NK = PEER_NKEYS
    nb = te // NK
    i1 = i1_ref[...]
    i2 = i2_ref[...]
    h = h_sc[...]
    for sub in range(nb):
        got = jnp.take_along_axis(s_sc[1 - slot, :, sub * NK:(sub + 1) * NK], i2, axis=1)
        h = jnp.where(i1 == (e - 1) * nb + sub, got, h)
    h_sc[...] = h
    s_sc[slot] = lax.dot_general(xn_ref[...], u_ref[...], _NT, preferred_element_type=f32)

    @pl.when(e == pl.num_programs(1) - 1)
    def _():
        act = 0.5 * h * (1.0 + lax.erf(h * (2.0 ** -0.5)))
        c_ref[...] = g_ref[...] * act


def peer_hidden(xn, u, i1, i2, g):
    M, D = xn.shape
    E = u.shape[0]
    S = PEER_SLOTS
    tm, te = _tile(M, 1024), 1024
    ne = E // te
    tok = lambda i, e: (i, 0)
    body = functools.partial(_peer_hidden_body, te=te)
    return pl.pallas_call(
        body,
        grid=(M // tm, ne + 1),
        in_specs=[pl.BlockSpec((tm, D), tok),
                  pl.BlockSpec((te, D), lambda i, e: (jnp.minimum(e, ne - 1), 0)),
                  pl.BlockSpec((tm, S), tok), pl.BlockSpec((tm, S), tok), pl.BlockSpec((tm, S), tok)],
        out_specs=pl.BlockSpec((tm, S), tok),
        out_shape=jax.ShapeDtypeStruct((M, S), f32),
        scratch_shapes=[pltpu.VMEM((tm, S), f32), pltpu.VMEM((2, tm, te), f32)],
        compiler_params=_params("parallel", "arbitrary"),
        name="peer_hidden",
    )(xn, u, i1, i2, g)


G_HALF = PEER_NKEYS // 2
G_PITCH = G_HALF + 4
G_UNROLL = 32
_HI16 = 0xFFFF0000


def _bf16_bits_high(x):
    return pltpu.bitcast(x.astype(bf16).astype(f32), jnp.uint32)


def _peer_out_body(i1_ref, i2_ref, c_ref, va_ref, vb_ref, x_ref, nw_ref, o_ref, g_sc, *, tm, nr, final_norm):
    e = pl.program_id(1)
    NK = PEER_NKEYS

    @pl.when(e == 0)
    def _():
        o_ref[...] = x_ref[...]
        sub = lax.broadcasted_iota(i32, (NK, PEER_SLOTS), 0)

        def tok_group(gi, carry):
            for u in range(G_UNROLL):
                n = gi * G_UNROLL + u
                r1 = i1_ref[pl.ds(n, 1), :]
                r2 = i2_ref[pl.ds(n, 1), :]
                rc = c_ref[pl.ds(n, 1), :]
                at = jnp.where(sub == r1, rc, 0.0).astype(bf16)
                bt = jnp.where(sub == r2, 1.0, 0.0).astype(bf16)
                tile = lax.dot_general(at, bt, _NT, preferred_element_type=f32)
                word = _bf16_bits_high(tile[0:G_HALF, :]) | (_bf16_bits_high(tile[G_HALF:NK, :]) >> 16)
                g_sc[pl.ds(n * G_PITCH, G_HALF), :] = word
            return carry

        lax.fori_loop(0, tm // G_UNROLL, tok_group, 0)

    his, los = [], []
    for s in range(nr):
        w = g_sc[pl.ds(e * nr + s, tm, stride=G_PITCH), :]
        his.append(pltpu.bitcast(w & jnp.uint32(_HI16), f32).astype(bf16))
        los.append(pltpu.bitcast(w << 16, f32).astype(bf16))
    acc = o_ref[...] + jnp.dot(jnp.concatenate(his, axis=1), va_ref[...], preferred_element_type=f32)
    acc = acc + jnp.dot(jnp.concatenate(los, axis=1), vb_ref[...], preferred_element_type=f32)

    if final_norm:
        last = e == pl.num_programs(1) - 1

        @pl.when(last)
        def _():
            o_ref[...] = _rms(acc, nw_ref[...])

        @pl.when(jnp.logical_not(last))
        def _():
            o_ref[...] = acc
    else:
        o_ref[...] = acc


def peer_out(i1, i2, c, v, x, final_nw=None):
    M, D = x.shape
    E = v.shape[0]
    S = PEER_SLOTS
    tm = _tile(M, 512)
    nr = 8
    te = nr * PEER_NKEYS
    ne = G_HALF // nr
    assert E == PEER_NKEYS * PEER_NKEYS and G_HALF % nr == 0
    tok = lambda i, e: (i, 0)
    nw = jnp.ones((D,), f32) if final_nw is None else final_nw
    body = functools.partial(_peer_out_body, tm=tm, nr=nr, final_norm=final_nw is not None)
    return pl.pallas_call(
        body,
        grid=(M // tm, ne),
        in_specs=[pl.BlockSpec((tm, S), tok), pl.BlockSpec((tm, S), tok), pl.BlockSpec((tm, S), tok),
                  pl.BlockSpec((te, D), lambda i, e: (e, 0)),
                  pl.BlockSpec((te, D), lambda i, e: (ne + e, 0)),
                  pl.BlockSpec((tm, D), tok),
                  pl.BlockSpec((1, D), lambda i, e: (0, 0))],
        out_specs=pl.BlockSpec((tm, D), tok),
        out_shape=jax.ShapeDtypeStruct((M, D), f32),
        scratch_shapes=[pltpu.VMEM((tm * G_PITCH, PEER_NKEYS), jnp.uint32)],
        compiler_params=_params("parallel", "arbitrary", vmem=VMEM_LIMIT_LARGE),
        name="peer_out",
    )(i1, i2, c, v, v, x, nw.reshape(1, D))


def peer_ffn(x, nw, wq_t, keys, u, v, final_nw=None):
    M, D = x.shape
    Mp = -(-M // LANES) * LANES
    xp = jnp.pad(x, ((0, Mp - M), (0, 0))) if Mp != M else x
    xn, i1, i2, g = peer_route(xp, nw, wq_t, keys)
    c = peer_hidden(xn, u, i1, i2, g)
    out = peer_out(i1, i2, c, v, xp, final_nw)
    return out[:M] if Mp != M else out


def _pad_time(a, B, T, Tp):
    if Tp == T:
        return a
    W = a.shape[1]
    return jnp.pad(a.reshape(B, T, W), ((0, 0), (0, Tp - T), (0, 0))).reshape(B * Tp, W)


def _unpad_time(a, B, T, Tp):
    if Tp == T:
        return a
    W = a.shape[1]
    return a.reshape(B, Tp, W)[:, :T].reshape(B * T, W)


def _trunk(x, past, hgrn0, ssm0, conv0, wts):
    B, T, D = x.shape
    M = B * T
    H = FOX_HEADS
    if T % SEQ_CHUNK == 0:
        Tp = T
    else:
        assert T <= SEQ_CHUNK
        Tp = SEQ_CHUNK
    x2 = x.reshape(M, D)

    proj = norm_matmul(x2, wts["norm_mix"][0], wts["w_in_a"], "in_proj_a")
    ff = proj[:, 7 * FOX_W:7 * FOX_W + H]
    if past is None:
        ff_t = ff.reshape(B, T, H).transpose(0, 2, 1).reshape(B * H, T)
        lf_t, c_t = fox_gate(ff_t, jnp.tile(wts["b_fox_f"], B).reshape(B * H, 1))
        flogf = lf_t.reshape(B, H, T).transpose(0, 2, 1)
        c_bth = c_t.reshape(B, H, T).transpose(0, 2, 1)
        fo, fk, fv = fox_prompt_attn(proj, c_bth, c_t.reshape(B * H, 1, T), B, T)
        fk = fk.reshape(B, T, H, FOX_HD)
        fv = fv.reshape(B, T, H, FOX_HD)
    else:
        fk = proj[:, FOX_W:2 * FOX_W].reshape(B, T, H, FOX_HD)
        fv = proj[:, 2 * FOX_W:3 * FOX_W].reshape(B, T, H, FOX_HD)
        assert T * H <= LANES
        k_pool, v_pool, lf_pool, page_table = past
        n_phys = k_pool.shape[0]
        pad = LANES - T * H
        ff_l = jnp.pad(ff.reshape(B, T * H), ((0, 0), (0, pad)))
        lf_l, c_l = fox_gate(ff_l, jnp.tile(wts["b_fox_f"], LANES // H).reshape(1, LANES), stride=H)
        flogf = lf_l[:, :T * H].reshape(B, T, H)
        sfx_pool, tot_pool = page_suffix_sums(lf_pool.reshape(n_phys, PAGE * H), H)
        q_r = proj[:, 0:FOX_W].reshape(B, T * H, FOX_HD).astype(bf16)
        k_new = jnp.pad(proj[:, FOX_W:2 * FOX_W].reshape(B, T * H, FOX_HD), ((0, 0), (0, pad), (0, 0)))
        v_new = jnp.pad(proj[:, 2 * FOX_W:3 * FOX_W].reshape(B, T * H, FOX_HD), ((0, 0), (0, pad), (0, 0)))
        fo = fox_decode_attn(q_r, k_new, v_new, c_l.reshape(B, 1, LANES),
                             k_pool.reshape(n_phys, PAGE * H, FOX_HD), v_pool.reshape(n_phys, PAGE * H, FOX_HD),
                             sfx_pool.reshape(n_phys, 1, PAGE * H), tot_pool.reshape(n_phys, 1, PAGE * H),
                             page_table).reshape(M, FOX_W)

    if Tp == T:
        hsrc, hcb0 = proj, 3 * H
    else:
        hsrc, hcb0 = _pad_time(proj[:, 3 * FOX_W:3 * FOX_W + 4 * HGRN_W], B, T, Tp), 0
    ho, h_state = gla(hsrc, hcb0, wts["hgrn_lb_logits"], wts["hgrn_norm_w"], hgrn0, B, Tp, T, 0)
    ho = _unpad_time(ho, B, T, Tp)
    x2 = matmul_residual([fo, ho], wts["w_out_a"], x2, "out_proj_a")
    x2 = peer_ffn(x2, wts["norm_ffn"][0], wts["peer_wq_t"][0], wts["peer_keys"][0], wts["peer_u"][0],
                  wts["peer_v"][0])

    projc = norm_matmul(x2, wts["norm_mix"][1], wts["w_in_c"], "in_proj_c")
    projc3 = projc.reshape(B, T, projc.shape[1])
    if T >= SSD_CONV - 1:
        conv_state = projc3[:, T - (SSD_CONV - 1):, SSD_INNER:SSD_INNER + SSD_CONV_DIM]
    else:
        conv_state = jnp.concatenate([conv0, projc3[:, :, SSD_INNER:SSD_INNER + SSD_CONV_DIM]],
                                     axis=1)[:, -(SSD_CONV - 1):]
    csrc = _pad_time(projc, B, T, Tp)
    conv0_p = jnp.pad(conv0, ((0, 0), (8 - (SSD_CONV - 1), 0), (0, 0)))
    yg, s_state = ssd(csrc, wts["conv_w"], wts["conv_b"], conv0_p, wts["dt_bias8"], wts["a_log8"],
                      wts["d_skip8"], wts["ssd_norm_w"], ssm0.reshape(B, SSD_GROUPS, SSD_GW, SSD_N), B, Tp, T)
    x2 = matmul_residual([_unpad_time(yg, B, T, Tp)], wts["w_out_c"], x2, "out_proj_c")
    y = peer_ffn(x2, wts["norm_ffn"][1], wts["peer_wq_t"][1], wts["peer_keys"][1], wts["peer_u"][1],
                 wts["peer_v"][1], final_nw=wts["norm_final"]).reshape(B, T, D)
    return (y, fk[None], fv[None], flogf[None], h_state[None],
            s_state.reshape(B, SSD_HEADS, SSD_HD, SSD_N)[None], conv_state[None])


def _row8(a):
    return jnp.tile(jnp.pad(a.astype(f32), (0, LANES - a.shape[0]))[None, :], (8, 1))


def kernel(x_prompt, x_sample, cache_fox_k, cache_fox_v, cache_fox_logf, state_hgrn, state_ssm, state_conv, page_table, norm_mix, norm_ffn, norm_final, w_in_a, b_fox_f, hgrn_lb_logits, hgrn_norm_w, w_out_a, w_in_c, conv_w, conv_b, dt_bias, a_log, d_skip, ssd_norm_w, w_out_c, peer_wq, peer_keys, peer_u, peer_v):
    D = x_prompt.shape[-1]
    wa = w_in_a[0]
    n_main = 3 * FOX_W
    wa_r = jnp.concatenate(
        [wa[:, :n_main], wa[:, n_main + FOX_HEADS:],
         jnp.pad(wa[:, n_main:n_main + FOX_HEADS], ((0, 0), (0, N_TAIL - FOX_HEADS)))], axis=1).astype(bf16)
    wc = w_in_c[0]
    wc_r = jnp.pad(wc, ((0, 0), (0, N_TAIL - SSD_HEADS))).astype(bf16)
    wts = dict(
        norm_mix=norm_mix, norm_ffn=norm_ffn, norm_final=norm_final,
        w_in_a=wa_r, b_fox_f=b_fox_f[0], hgrn_lb_logits=hgrn_lb_logits, hgrn_norm_w=hgrn_norm_w[0],
        w_out_a=w_out_a[0].astype(bf16),
        w_in_c=wc_r, conv_w=conv_w[0], conv_b=conv_b[0],
        dt_bias8=_row8(dt_bias[0]), a_log8=_row8(a_log[0]), d_skip8=_row8(d_skip[0]),
        ssd_norm_w=ssd_norm_w[0], w_out_c=w_out_c[0].astype(bf16),
        peer_wq_t=jnp.swapaxes(peer_wq, 1, 2).astype(bf16), peer_keys=peer_keys.astype(bf16),
        peer_u=peer_u.astype(bf16), peer_v=peer_v.astype(bf16),
    )
    Bp = x_prompt.shape[0]
    hgrn0_p = jnp.zeros((Bp, HGRN_HEADS, HGRN_DK, HGRN_DV), f32)
    ssm0_p = jnp.zeros((Bp, SSD_HEADS, SSD_HD, SSD_N), f32)
    conv0_p = jnp.zeros((Bp, SSD_CONV - 1, SSD_CONV_DIM), f32)
    y_p, fk_p, fv_p, fl_p, h_p, s_p, c_p = _trunk(x_prompt, None, hgrn0_p, ssm0_p, conv0_p, wts)
    past = (cache_fox_k[0], cache_fox_v[0], cache_fox_logf[0], page_table)
    y_s, fk_s, fv_s, fl_s, h_s, s_s, c_s = _trunk(x_sample, past, state_hgrn[0], state_ssm[0], state_conv[0], wts)
    return (y_p, y_s, fk_p, fv_p, fl_p, fk_s, fv_s, fl_s, h_p, h_s, s_p, s_s, c_p, c_s)
```
